```python
import math
import jax, jax.numpy as jnp
from jax import lax
import numpy as np

D_MODEL = 2048
BATCH = 2
SEQ = 4096
DEPTH = 2
DEC_BATCH = 32
DEC_SEQ = 4
PAST_LEN = 8192
PAGE_SIZE = 128

N_MIXERS = 2
N_ATTN_LAYERS = (DEPTH + 1) // 2
N_HGRN_LAYERS = DEPTH // 2

DA_HEADS = 8
DA_HEAD_DIM = D_MODEL // (2 * DA_HEADS)
DA_V_DIM = 2 * DA_HEAD_DIM
DA_QK_WIDTH = DA_HEADS * 2 * DA_HEAD_DIM
DA_V_WIDTH = DA_HEADS * DA_V_DIM
ROPE_DIM = DA_HEAD_DIM // 4
ROPE_THETA = 500000.0
Q_BLOCK = 128

HG_HEADS = 16
HG_DK = D_MODEL // HG_HEADS
HG_DV = D_MODEL // HG_HEADS
HG_K_WIDTH = HG_HEADS * HG_DK
HG_V_WIDTH = HG_HEADS * HG_DV
HG_CHUNK = 64

EPS = 1e-6
F32 = jnp.float32

kernel_name = 'diffattn_hgrn2_hybrid_step'


def rms_norm(x, gain):
    xf = x.astype(F32)
    y = xf * lax.rsqrt(jnp.mean(xf * xf, axis=-1, keepdims=True) + EPS)
    return (y * gain.astype(F32)).astype(x.dtype)


def lambda_init(layer_idx):
    return 0.8 - 0.6 * math.exp(-0.3 * layer_idx)


def partial_rope(x, pos):
    inv = jnp.power(ROPE_THETA, -jnp.arange(0, ROPE_DIM, 2, dtype=F32) / ROPE_DIM)
    ang = pos.astype(F32)[:, None] * inv[None, :]
    cos = jnp.cos(ang)[:, None, None, :]
    sin = jnp.sin(ang)[:, None, None, :]
    xr = x[..., :ROPE_DIM].astype(F32)
    x1, x2 = xr[..., :ROPE_DIM // 2], xr[..., ROPE_DIM // 2:]
    rot = jnp.concatenate([x1 * cos - x2 * sin, x2 * cos + x1 * sin], axis=-1)
    return jnp.concatenate([rot.astype(x.dtype), x[..., ROPE_DIM:]], axis=-1)


def da_project(x, norm_g, w_in, q_g, k_g, pos):
    lead = x.shape[:-1]
    p = rms_norm(x, norm_g) @ w_in
    q, k, v, gate = jnp.split(p, [DA_QK_WIDTH, 2 * DA_QK_WIDTH, 2 * DA_QK_WIDTH + DA_V_WIDTH], axis=-1)
    q = q.reshape(*lead, DA_HEADS, 2, DA_HEAD_DIM)
    k = k.reshape(*lead, DA_HEADS, 2, DA_HEAD_DIM)
    q = partial_rope(rms_norm(q, q_g), pos)
    k = partial_rope(rms_norm(k, k_g), pos)
    v = v.reshape(*lead, DA_HEADS, DA_V_DIM)
    return q, k, v, gate


def diff_attend(q, k, v, mask, lam):
    s = jnp.einsum('...qhcd,...khcd->...hcqk', q, k).astype(F32) * (DA_HEAD_DIM ** -0.5)
    s = jnp.where(mask, s, -jnp.inf)
    a = jax.nn.softmax(s, axis=-1)
    a = a[..., 0, :, :] - lam * a[..., 1, :, :]
    return jnp.einsum('...hqk,...khe->...qhe', a.astype(v.dtype), v)


def da_prompt(q, k, v, lam):
    b, t = q.shape[:2]
    nb = t // Q_BLOCK
    qb = jnp.moveaxis(q.reshape(b, nb, Q_BLOCK, DA_HEADS, 2, DA_HEAD_DIM), 1, 0)
    kpos = jnp.arange(t)

    def block(args):
        qblk, start = args
        qpos = start + jnp.arange(Q_BLOCK)
        mask = kpos[None, :] <= qpos[:, None]
        return diff_attend(qblk, k, v, mask, lam)

    ob = lax.map(block, (qb, jnp.arange(nb) * Q_BLOCK))
    return jnp.moveaxis(ob, 0, 1).reshape(b, t, DA_HEADS, DA_V_DIM)


def da_sample(q, k_new, v_new, cache_k, cache_v, layer, page_table, lam):
    s_len = q.shape[1]
    kpos = jnp.arange(PAST_LEN + s_len)
    qpos = PAST_LEN + jnp.arange(s_len)
    mask = kpos[None, :] <= qpos[:, None]

    def one(args):
        pages, qs, ks, vs = args
        kp = cache_k[layer, pages].reshape(-1, DA_HEADS, 2, DA_HEAD_DIM)
        vp = cache_v[layer, pages].reshape(-1, DA_HEADS, DA_V_DIM)
        kk = jnp.concatenate([kp.astype(ks.dtype), ks], axis=0)
        vv = jnp.concatenate([vp.astype(vs.dtype), vs], axis=0)
        return diff_attend(qs, kk, vv, mask, lam)

    return lax.map(one, (page_table, q, k_new, v_new))


def da_output(o, gate, subln_g, lam_init, w_out):
    o = rms_norm(o, subln_g) * (1.0 - lam_init)
    o = o.reshape(*o.shape[:-2], DA_V_WIDTH) * jax.nn.silu(gate)
    return o @ w_out


def hgrn_lower_bound(lb_param, layer_idx):
    p = jax.nn.softmax(lb_param.astype(F32), axis=0)
    return jnp.cumsum(p, axis=0)[layer_idx] - p[0]


def hgrn_project(x, norm_g, w_in, lb):
    lead = x.shape[:-1]
    p = rms_norm(x, norm_g) @ w_in
    q, f, i, gate = jnp.split(p, [HG_K_WIDTH, 2 * HG_K_WIDTH, 2 * HG_K_WIDTH + HG_V_WIDTH], axis=-1)
    q = jax.nn.silu(q.astype(F32)).reshape(*lead, HG_HEADS, HG_DK)
    fg = lb + (1.0 - lb) * jax.nn.sigmoid(f.astype(F32))
    k = (1.0 - fg).reshape(*lead, HG_HEADS, HG_DK)
    logf = jnp.log(fg).reshape(*lead, HG_HEADS, HG_DK)
    i = i.astype(F32).reshape(*lead, HG_HEADS, HG_DV)
    return q, k, i, logf, gate


def hgrn_chunk(s0, q, k, v, logf):
    l = q.shape[1]
    b = jnp.cumsum(logf, axis=1)
    o_inter = jnp.einsum('blhk,bhkv->blhv', q * jnp.exp(b), s0)
    causal = jnp.tril(jnp.ones((l, l), dtype=bool))
    diff = b[:, :, None] - b[:, None, :]
    dec = jnp.exp(jnp.where(causal[None, :, :, None, None], diff, -jnp.inf))
    a = jnp.einsum('bthk,bshk,btshk->bhts', q, k, dec)
    o_intra = jnp.einsum('bhts,bshv->bthv', a, v)
    b_last = b[:, -1]
    s1 = jnp.exp(b_last)[..., None] * s0 + jnp.einsum('blhk,blhv->bhkv', k * jnp.exp(b_last[:, None] - b), v)
    return s1, o_inter + o_intra


def hgrn_prompt(q, k, v, logf):
    b, t = q.shape[:2]
    nc = t // HG_CHUNK

    def to_chunks(a):
        return jnp.moveaxis(a.reshape(b, nc, HG_CHUNK, *a.shape[2:]), 1, 0)

    s0 = jnp.zeros((b, HG_HEADS, HG_DK, HG_DV), F32)

    def step(s, xs):
        return hgrn_chunk(s, *xs)

    s_fin, o = lax.scan(step, s0, (to_chunks(q), to_chunks(k), to_chunks(v), to_chunks(logf)))
    o = jnp.moveaxis(o, 0, 1).reshape(b, t, HG_HEADS, HG_DV)
    return o, s_fin


def hgrn_output(o, gate, out_g, w_out):
    o = o.reshape(*o.shape[:-2], HG_V_WIDTH)
    o = rms_norm(o, out_g).astype(gate.dtype) * jax.nn.silu(gate)
    return o @ w_out


def setup_inputs(seed: int = 0) -> dict:
    key = jax.random.key(seed)
    ks = jax.random.split(key, 20)
    n_pages = PAST_LEN // PAGE_SIZE
    n_used = DEC_BATCH * n_pages
    n_pool = n_used + n_used // 4 + 1
    nrm = jax.random.normal
    x_prompt = nrm(ks[0], (BATCH, SEQ, D_MODEL), F32)
    x_sample = nrm(ks[1], (DEC_BATCH, DEC_SEQ, D_MODEL), F32)
    cache_k = nrm(ks[2], (N_ATTN_LAYERS, n_pool, PAGE_SIZE, DA_HEADS, 2 * DA_HEAD_DIM), F32)
    cache_v = nrm(ks[3], (N_ATTN_LAYERS, n_pool, PAGE_SIZE, DA_HEADS, DA_V_DIM), F32)
    state_hgrn = 0.3 * nrm(ks[4], (N_HGRN_LAYERS, DEC_BATCH, HG_HEADS, HG_DK, HG_DV), F32)
    page_table = jax.random.permutation(ks[5], n_pool)[:n_used].reshape(DEC_BATCH, n_pages).astype(jnp.int32)
    attn_norm = 1.0 + 0.02 * nrm(ks[6], (N_ATTN_LAYERS, D_MODEL), F32)
    attn_w_in = nrm(ks[7], (N_ATTN_LAYERS, D_MODEL, 2 * DA_QK_WIDTH + 2 * DA_V_WIDTH), F32) * D_MODEL ** -0.5
    attn_q_norm = 1.0 + 0.02 * nrm(ks[8], (N_ATTN_LAYERS, 2, DA_HEAD_DIM), F32)
    attn_k_norm = 1.0 + 0.02 * nrm(ks[9], (N_ATTN_LAYERS, 2, DA_HEAD_DIM), F32)
    attn_lambda = 0.1 * nrm(ks[10], (N_ATTN_LAYERS, 4, DA_HEAD_DIM), F32)
    attn_subln = 1.0 + 0.02 * nrm(ks[11], (N_ATTN_LAYERS, DA_V_DIM), F32)
    attn_w_out = nrm(ks[12], (N_ATTN_LAYERS, DA_V_WIDTH, D_MODEL), F32) * DA_V_WIDTH ** -0.5
    hgrn_norm = 1.0 + 0.02 * nrm(ks[13], (N_HGRN_LAYERS, D_MODEL), F32)
    hgrn_w_in = nrm(ks[14], (N_HGRN_LAYERS, D_MODEL, 2 * HG_K_WIDTH + 2 * HG_V_WIDTH), F32) * D_MODEL ** -0.5
    hgrn_lower_bounds = 0.5 * nrm(ks[15], (DEPTH, HG_K_WIDTH), F32)
    hgrn_out_norm = 1.0 + 0.02 * nrm(ks[16], (N_HGRN_LAYERS, HG_V_WIDTH), F32)
    hgrn_w_out = nrm(ks[17], (N_HGRN_LAYERS, HG_V_WIDTH, D_MODEL), F32) * HG_V_WIDTH ** -0.5
    return {'x_prompt': x_prompt, 'x_sample': x_sample, 'cache_k': cache_k, 'cache_v': cache_v,
            'state_hgrn': state_hgrn, 'page_table': page_table,
            'attn_norm': attn_norm, 'attn_w_in': attn_w_in, 'attn_q_norm': attn_q_norm,
            'attn_k_norm': attn_k_norm, 'attn_lambda': attn_lambda, 'attn_subln': attn_subln,
            'attn_w_out': attn_w_out, 'hgrn_norm': hgrn_norm, 'hgrn_w_in': hgrn_w_in,
            'hgrn_lower_bounds': hgrn_lower_bounds, 'hgrn_out_norm': hgrn_out_norm, 'hgrn_w_out': hgrn_w_out}


def reference(x_prompt, x_sample, cache_k, cache_v, state_hgrn, page_table, attn_norm, attn_w_in, attn_q_norm,
              attn_k_norm, attn_lambda, attn_subln, attn_w_out, hgrn_norm, hgrn_w_in, hgrn_lower_bounds,
              hgrn_out_norm, hgrn_w_out):
    pos_p = jnp.arange(x_prompt.shape[1])
    pos_s = PAST_LEN + jnp.arange(x_sample.shape[1])
    xp, xs = x_prompt, x_sample
    kp_l, vp_l, ks_l, vs_l, sp_l, ss_l = [], [], [], [], [], []
    for i in range(DEPTH):
        if i % N_MIXERS == 0:
            a = i // N_MIXERS
            lq1, lk1, lq2, lk2 = attn_lambda[a].astype(F32)
            lam0 = lambda_init(i)
            lam = jnp.exp(jnp.sum(lq1 * lk1)) - jnp.exp(jnp.sum(lq2 * lk2)) + lam0
            q, k, v, g = da_project(xp, attn_norm[a], attn_w_in[a], attn_q_norm[a], attn_k_norm[a], pos_p)
            o = da_prompt(q, k, v, lam)
            xp = xp + da_output(o, g, attn_subln[a], lam0, attn_w_out[a])
            kp_l.append(k.reshape(*k.shape[:-2], 2 * DA_HEAD_DIM))
            vp_l.append(v)
            q, k, v, g = da_project(xs, attn_norm[a], attn_w_in[a], attn_q_norm[a], attn_k_norm[a], pos_s)
            o = da_sample(q, k, v, cache_k, cache_v, a, page_table, lam)
            xs = xs + da_output(o, g, attn_subln[a], lam0, attn_w_out[a])
            ks_l.append(k.reshape(*k.shape[:-2], 2 * DA_HEAD_DIM))
            vs_l.append(v)
        else:
            r = i // N_MIXERS
            lb = hgrn_lower_bound(hgrn_lower_bounds, i)
            q, k, v, logf, g = hgrn_project(xp, hgrn_norm[r], hgrn_w_in[r], lb)
            o, s_fin = hgrn_prompt(q, k, v, logf)
            xp = xp + hgrn_output(o, g, hgrn_out_norm[r], hgrn_w_out[r])
            sp_l.append(s_fin)
            q, k, v, logf, g = hgrn_project(xs, hgrn_norm[r], hgrn_w_in[r], lb)
            s_new, o = hgrn_chunk(state_hgrn[r].astype(F32), q, k, v, logf)
            xs = xs + hgrn_output(o, g, hgrn_out_norm[r], hgrn_w_out[r])
            ss_l.append(s_new)
    k_prompt = jnp.stack(kp_l)
    v_prompt = jnp.stack(vp_l)
    k_sample = jnp.stack(ks_l)
    v_sample = jnp.stack(vs_l)
    s_prompt = jnp.stack(sp_l)
    s_sample = jnp.stack(ss_l)
    return (xp, xs, k_prompt, v_prompt, k_sample, v_sample, s_prompt, s_sample)
```

```python
import functools
import math

import jax
import jax.numpy as jnp
from jax import lax
from jax.experimental import pallas as pl
from jax.experimental.pallas import tpu as pltpu

F32 = jnp.float32
BF16 = jnp.bfloat16

D_MODEL = 2048
PAST_LEN = 8192
PAGE_SIZE = 128
DA_HEADS = 8
DA_HEAD_DIM = 128
DA_V_DIM = 256
ROPE_DIM = 32
ROPE_THETA = 500000.0
HG_HEADS = 16
HG_DK = 128
EPS = 1e-6
LAMBDA_INIT_0 = 0.8 - 0.6 * math.exp(-0.3 * 0)

LANES = 128
SUBLANES = 8
VMEM_LIMIT = 56 * 1024 * 1024
NEG_BIG = -1e30
SM_SCALE = DA_HEAD_DIM ** -0.5
EXP2_SCALE = SM_SCALE * math.log2(math.e)

NT_DIMS = (((1,), (1,)), ((), ()))
TN_DIMS = (((0,), (0,)), ((), ()))


def _silu(x):
    return x * (1.0 / (1.0 + jnp.exp(-x)))


def _sigmoid(x):
    return 1.0 / (1.0 + jnp.exp(-x))


def _group_map(group, nblk):
    return lambda i, j: (i, jnp.clip(j - group * nblk, 0, nblk - 1))


def _norm_rows(x_ref, g_ref, xn_ref):
    x = x_ref[...]
    ms = jnp.mean(x * x, axis=-1, keepdims=True)
    xn_ref[...] = (x * lax.rsqrt(ms + EPS) * g_ref[...]).astype(BF16)


def _attn_proj_kernel(x_ref, g_ref, w_ref, qg_ref, kg_ref, cos_ref, sa_ref, sb_ref,
                      q_ref, k_ref, kb_ref, v_ref, vb_ref, gate_ref, xn_ref, *, tn, nblk):
    j = pl.program_id(1)

    @pl.when(j == 0)
    def _():
        _norm_rows(x_ref, g_ref, xn_ref)

    acc = jnp.dot(xn_ref[...], w_ref[...], preferred_element_type=F32)

    def qk_post(gain_ref):
        cols = []
        for idx in range(tn // LANES):
            c = idx % 2
            a = acc[:, idx * LANES:(idx + 1) * LANES]
            ms = jnp.mean(a * a, axis=-1, keepdims=True)
            y = a * lax.rsqrt(ms + EPS) * gain_ref[c:c + 1, :]
            y = (y * cos_ref[...] + pltpu.roll(y, ROPE_DIM // 2, 1) * sa_ref[...]
                 + pltpu.roll(y, LANES - ROPE_DIM // 2, 1) * sb_ref[...])
            cols.append(y)
        return cols

    @pl.when(j < nblk)
    def _():
        for idx, y in enumerate(qk_post(qg_ref)):
            q_ref[:, idx * LANES:(idx + 1) * LANES] = y.astype(BF16)

    @pl.when((j >= nblk) & (j < 2 * nblk))
    def _():
        for idx, y in enumerate(qk_post(kg_ref)):
            k_ref[:, idx * LANES:(idx + 1) * LANES] = y
            kb_ref[:, idx * LANES:(idx + 1) * LANES] = y.astype(BF16)

    @pl.when((j >= 2 * nblk) & (j < 3 * nblk))
    def _():
        v_ref[...] = acc
        vb_ref[...] = acc.astype(BF16)

    @pl.when(j >= 3 * nblk)
    def _():
        gate_ref[...] = acc


def _attn_proj(x, norm_g, w_bf, q_g, k_g, cos_t, sa_t, sb_t, *, tm, tn):
    m = x.shape[0]
    nblk = D_MODEL // tn
    tab_blocks = cos_t.shape[0] // tm
    tab_spec = pl.BlockSpec((tm, LANES), lambda i, j: (i % tab_blocks, 0))
    small = lambda shape: pl.BlockSpec(shape, lambda i, j: (0, 0))
    out_sd = lambda dt: jax.ShapeDtypeStruct((m, D_MODEL), dt)
    out_spec = lambda g: pl.BlockSpec((tm, tn), _group_map(g, nblk))
    return pl.pallas_call(
        functools.partial(_attn_proj_kernel, tn=tn, nblk=nblk),
        out_shape=(out_sd(BF16), out_sd(F32), out_sd(BF16), out_sd(F32), out_sd(BF16), out_sd(F32)),
        grid=(m // tm, 4 * nblk),
        in_specs=[pl.BlockSpec((tm, D_MODEL), lambda i, j: (i, 0)),
                  small((1, D_MODEL)),
                  pl.BlockSpec((D_MODEL, tn), lambda i, j: (0, j)),
                  small((2, LANES)), small((2, LANES)),
                  tab_spec, tab_spec, tab_spec],
        out_specs=(out_spec(0), out_spec(1), out_spec(1), out_spec(2), out_spec(2), out_spec(3)),
        scratch_shapes=[pltpu.VMEM((tm, D_MODEL), BF16)],
        compiler_params=pltpu.CompilerParams(
            dimension_semantics=("arbitrary", "arbitrary"), vmem_limit_bytes=VMEM_LIMIT),
        name="attn_proj",
    )(x, norm_g, w_bf, q_g, k_g, cos_t, sa_t, sb_t)


def _hgrn_proj_kernel(x_ref, g_ref, w_ref, lb_ref, q_ref, k_ref, lf_ref, i_ref, gate_ref, xn_ref, *, nblk):
    j = pl.program_id(1)

    @pl.when(j == 0)
    def _():
        _norm_rows(x_ref, g_ref, xn_ref)

    acc = jnp.dot(xn_ref[...], w_ref[...], preferred_element_type=F32)

    @pl.when(j < nblk)
    def _():
        q_ref[...] = _silu(acc)

    @pl.when((j >= nblk) & (j < 2 * nblk))
    def _():
        lb = lb_ref[...]
        fg = lb + (1.0 - lb) * _sigmoid(acc)
        k_ref[...] = 1.0 - fg
        lf_ref[...] = jnp.log(fg)

    @pl.when((j >= 2 * nblk) & (j < 3 * nblk))
    def _():
        i_ref[...] = acc

    @pl.when(j >= 3 * nblk)
    def _():
        gate_ref[...] = acc


def _hgrn_proj(x, norm_g, w_bf, lb, *, tm, tn):
    m = x.shape[0]
    nblk = D_MODEL // tn
    out_sd = jax.ShapeDtypeStruct((m, D_MODEL), F32)
    out_spec = lambda g: pl.BlockSpec((tm, tn), _group_map(g, nblk))
    return pl.pallas_call(
        functools.partial(_hgrn_proj_kernel, nblk=nblk),
        out_shape=(out_sd,) * 5,
        grid=(m // tm, 4 * nblk),
        in_specs=[pl.BlockSpec((tm, D_MODEL), lambda i, j: (i, 0)),
                  pl.BlockSpec((1, D_MODEL), lambda i, j: (0, 0)),
                  pl.BlockSpec((D_MODEL, tn), lambda i, j: (0, j)),
                  pl.BlockSpec((1, tn), lambda i, j: (0, jnp.clip(j - nblk, 0, nblk - 1)))],
        out_specs=(out_spec(0), out_spec(1), out_spec(1), out_spec(2), out_spec(3)),
        scratch_shapes=[pltpu.VMEM((tm, D_MODEL), BF16)],
        compiler_params=pltpu.CompilerParams(
            dimension_semantics=("arbitrary", "arbitrary"), vmem_limit_bytes=VMEM_LIMIT),
        name="hgrn_proj",
    )(x, norm_g, w_bf, lb)


def _out_proj_kernel(a_ref, w_ref, x_ref, y_ref):
    y_ref[...] = x_ref[...] + jnp.dot(a_ref[...], w_ref[...], preferred_element_type=F32)


def _out_proj(a_bf, w_bf, x, *, tm):
    m = x.shape[0]
    row = pl.BlockSpec((tm, D_MODEL), lambda i: (i, 0))
    return pl.pallas_call(
        _out_proj_kernel,
        out_shape=jax.ShapeDtypeStruct((m, D_MODEL), F32),
        grid=(m // tm,),
        in_specs=[row, pl.BlockSpec((D_MODEL, D_MODEL), lambda i: (0, 0)), row],
        out_specs=row,
        compiler_params=pltpu.CompilerParams(
            dimension_semantics=("arbitrary",), vmem_limit_bytes=VMEM_LIMIT),
        name="attn_out_proj",
    )(a_bf, w_bf, x)


def _hgrn_out_kernel(o_ref, gate_ref, g_ref, w_ref, x_ref, y_ref):
    o = o_ref[...]
    ms = jnp.mean(o * o, axis=-1, keepdims=True)
    a = (o * lax.rsqrt(ms + EPS) * g_ref[...]) * _silu(gate_ref[...])
    y_ref[...] = x_ref[...] + jnp.dot(a.astype(BF16), w_ref[...], preferred_element_type=F32)


def _hgrn_out(o, gate, out_g, w_bf, x, *, tm):
    m = x.shape[0]
    row = pl.BlockSpec((tm, D_MODEL), lambda i: (i, 0))
    return pl.pallas_call(
        _hgrn_out_kernel,
        out_shape=jax.ShapeDtypeStruct((m, D_MODEL), F32),
        grid=(m // tm,),
        in_specs=[row, row, pl.BlockSpec((1, D_MODEL), lambda i: (0, 0)),
                  pl.BlockSpec((D_MODEL, D_MODEL), lambda i: (0, 0)), row],
        out_specs=row,
        compiler_params=pltpu.CompilerParams(
            dimension_semantics=("arbitrary",), vmem_limit_bytes=VMEM_LIMIT),
        name="hgrn_out_proj",
    )(o, gate, out_g, w_bf, x)


def _online_softmax_step(s, v_bf, m_ref, l_ref, acc_ref, rows, scale_in_exp2):
    m_prev = m_ref[rows, :]
    m_new = jnp.maximum(m_prev, jnp.max(s, axis=-1, keepdims=True))
    if scale_in_exp2:
        alpha = jnp.exp2((m_prev - m_new) * EXP2_SCALE)
        p = jnp.exp2((s - m_new) * EXP2_SCALE)
    else:
        alpha = jnp.exp(m_prev - m_new)
        p = jnp.exp(s - m_new)
    l_ref[rows, :] = alpha * l_ref[rows, :] + jnp.sum(p, axis=-1, keepdims=True)
    acc_ref[rows, :] = alpha * acc_ref[rows, :] + jnp.dot(p.astype(BF16), v_bf, preferred_element_type=F32)
    m_ref[rows, :] = m_new


def _subln_gate(o, gate, sg_ref):
    ms = jnp.mean(o * o, axis=-1, keepdims=True)
    o = (o * lax.rsqrt(ms + EPS) * sg_ref[...]) * (1.0 - LAMBDA_INIT_0)
    return o * _silu(gate)


def _flash_kernel(lam_ref, q_ref, k_ref, v_ref, gate_ref, sg_ref, o_ref, m_ref, l_ref, acc_ref, *, tq):
    qi = pl.program_id(2)
    m_ref[...] = jnp.full(m_ref.shape, NEG_BIG, F32)
    l_ref[...] = jnp.zeros(l_ref.shape, F32)
    acc_ref[...] = jnp.zeros(acc_ref.shape, F32)

    def step(ki, masked):
        kv_rows = pl.ds(pl.multiple_of(ki * tq, tq), tq)
        v_bf = v_ref[kv_rows, :]
        for c in range(2):
            comp = slice(c * LANES, (c + 1) * LANES)
            s = lax.dot_general(q_ref[:, comp], k_ref[kv_rows, comp], NT_DIMS, preferred_element_type=F32)
            if masked:
                r = lax.broadcasted_iota(jnp.int32, (tq, tq), 0)
                col = lax.broadcasted_iota(jnp.int32, (tq, tq), 1)
                s = jnp.where(col <= r, s, NEG_BIG)
            _online_softmax_step(s, v_bf, m_ref, l_ref, acc_ref, slice(c * tq, (c + 1) * tq), True)

    def body(ki, carry):
        step(ki, False)
        return carry

    lax.fori_loop(0, qi, body, 0)
    step(qi, True)

    lam = lam_ref[0]
    o = acc_ref[0:tq, :] / l_ref[0:tq, :] - lam * (acc_ref[tq:2 * tq, :] / l_ref[tq:2 * tq, :])
    o_ref[...] = _subln_gate(o, gate_ref[...], sg_ref).astype(BF16)


def _flash_attention(lam, q_bf, k_bf, v_bf, gate, subln_g, *, batch, seq, tq):
    nq = seq // tq
    qmap = lambda b, h, qi: (b * nq + qi, h)
    kvmap = lambda b, h, qi: (b, h)
    return pl.pallas_call(
        functools.partial(_flash_kernel, tq=tq),
        out_shape=jax.ShapeDtypeStruct((batch * seq, DA_HEADS * DA_V_DIM), BF16),
        grid=(batch, DA_HEADS, nq),
        in_specs=[pl.BlockSpec(memory_space=pltpu.SMEM),
                  pl.BlockSpec((tq, DA_V_DIM), qmap),
                  pl.BlockSpec((seq, DA_V_DIM), kvmap),
                  pl.BlockSpec((seq, DA_V_DIM), kvmap),
                  pl.BlockSpec((tq, DA_V_DIM), qmap),
                  pl.BlockSpec((1, DA_V_DIM), lambda b, h, qi: (0, 0))],
        out_specs=pl.BlockSpec((tq, DA_V_DIM), qmap),
        scratch_shapes=[pltpu.VMEM((2 * tq, 1), F32), pltpu.VMEM((2 * tq, 1), F32),
                        pltpu.VMEM((2 * tq, DA_V_DIM), F32)],
        compiler_params=pltpu.CompilerParams(
            dimension_semantics=("arbitrary", "arbitrary", "arbitrary"), vmem_limit_bytes=VMEM_LIMIT),
        name="flash_diff_attn",
    )(lam, q_bf, k_bf, v_bf, gate, subln_g)


SAMPLE_ROWS = 64
PAGE_ROWS = PAGE_SIZE * DA_HEADS


def _sample_attn_kernel(pt_ref, lam_ref, qm_ref, kc_ref, vc_ref, kn_ref, vn_ref, gate_ref, sg_ref,
                        o_ref, m_ref, l_ref, acc_ref, *, n_pages, dec_seq):
    p = pl.program_id(1)

    @pl.when(p == 0)
    def _():
        m_ref[...] = jnp.full(m_ref.shape, NEG_BIG, F32)
        l_ref[...] = jnp.zeros(l_ref.shape, F32)
        acc_ref[...] = jnp.zeros(acc_ref.shape, F32)

    def block(kf, vf, causal):
        ncols = kf.shape[0]
        s = lax.dot_general(qm_ref[0], kf.astype(BF16), NT_DIMS, preferred_element_type=F32) * SM_SCALE
        r = lax.broadcasted_iota(jnp.int32, (SAMPLE_ROWS, ncols), 0)
        col = lax.broadcasted_iota(jnp.int32, (SAMPLE_ROWS, ncols), 1)
        valid = (col % DA_HEADS) == ((r // dec_seq) % DA_HEADS)
        if causal:
            valid = valid & ((col // DA_HEADS) <= (r % dec_seq))
        s = jnp.where(valid, s, NEG_BIG)
        _online_softmax_step(s, vf.astype(BF16), m_ref, l_ref, acc_ref, slice(0, SAMPLE_ROWS), False)

    block(kc_ref[0, 0].reshape(PAGE_ROWS, DA_V_DIM), vc_ref[0, 0].reshape(PAGE_ROWS, DA_V_DIM), False)

    @pl.when(p == n_pages - 1)
    def _():
        block(kn_ref[0], vn_ref[0], True)
        half = SAMPLE_ROWS // 2
        lam = lam_ref[0]
        o = acc_ref[0:half, :] / l_ref[0:half, :] - lam * (acc_ref[half:, :] / l_ref[half:, :])
        o_ref[0] = _subln_gate(o, gate_ref[0], sg_ref).astype(BF16)


def _sample_attention(page_table, lam, qm, cache_k, cache_v, k_new, v_new, gate_r, subln_g, *, dec_seq):
    dec_batch, n_pages = page_table.shape
    rows_new = dec_seq * DA_HEADS
    page_spec = pl.BlockSpec((1, 1, PAGE_SIZE, DA_HEADS, DA_V_DIM), lambda b, p, pt: (0, pt[b, p], 0, 0, 0))
    per_b = lambda rows, cols: pl.BlockSpec((1, rows, cols), lambda b, p, pt: (b, 0, 0))
    grid_spec = pltpu.PrefetchScalarGridSpec(
        num_scalar_prefetch=1,
        grid=(dec_batch, n_pages),
        in_specs=[pl.BlockSpec(memory_space=pltpu.SMEM),
                  per_b(SAMPLE_ROWS, DA_V_DIM), page_spec, page_spec,
                  per_b(rows_new, DA_V_DIM), per_b(rows_new, DA_V_DIM), per_b(rows_new, DA_V_DIM),
                  pl.BlockSpec((1, DA_V_DIM), lambda b, p, pt: (0, 0))],
        out_specs=per_b(rows_new, DA_V_DIM),
        scratch_shapes=[pltpu.VMEM((SAMPLE_ROWS, 1), F32), pltpu.VMEM((SAMPLE_ROWS, 1), F32),
                        pltpu.VMEM((SAMPLE_ROWS, DA_V_DIM), F32)],
    )
    return pl.pallas_call(
        functools.partial(_sample_attn_kernel, n_pages=n_pages, dec_seq=dec_seq),
        out_shape=jax.ShapeDtypeStruct((dec_batch, rows_new, DA_V_DIM), BF16),
        grid_spec=grid_spec,
        compiler_params=pltpu.CompilerParams(
            dimension_semantics=("arbitrary", "arbitrary"), vmem_limit_bytes=VMEM_LIMIT),
        name="paged_diff_attn",
    )(page_table, lam, qm, cache_k, cache_v, k_new, v_new, gate_r, subln_g)


def _tile_cumsum(x, rowi):
    for s in (1, 2, 4):
        x = x + jnp.where(rowi >= s, pltpu.roll(x, s, 0), 0.0)
    return x


def _hgrn_chunk(q, k, v, lf, st):
    length = q.shape[0]
    nt = length // SUBLANES
    rowi = lax.broadcasted_iota(jnp.int32, (SUBLANES, LANES), 0)
    tile = lambda a, r: a[r * SUBLANES:(r + 1) * SUBLANES, :]

    b_tiles = []
    carry = None
    for r in range(nt):
        c = _tile_cumsum(tile(lf, r), rowi)
        if carry is not None:
            c = c + carry
        carry = c[SUBLANES - 1:SUBLANES, :]
        b_tiles.append(c)
    b = jnp.concatenate(b_tiles, axis=0) if nt > 1 else b_tiles[0]
    b_last = carry

    qe = (q * jnp.exp(b)).astype(BF16)
    o = lax.dot_general(qe, st.astype(BF16), NT_DIMS, preferred_element_type=F32)
    kd = (k * jnp.exp(b_last - b)).astype(BF16)
    st_new = st * jnp.exp(b_last) + lax.dot_general(v.astype(BF16), kd, TN_DIMS, preferred_element_type=F32)

    o_tiles = []
    for r in range(nt):
        bt, qt, kt, vt = b_tiles[r], tile(q, r), tile(k, r), tile(v, r)
        ot = jnp.sum(qt * kt, axis=-1, keepdims=True) * vt
        for d in range(1, SUBLANES):
            dec = jnp.exp(jnp.minimum(bt - pltpu.roll(bt, d, 0), 0.0))
            a = jnp.sum(qt * pltpu.roll(kt, d, 0) * dec, axis=-1, keepdims=True)
            a = jnp.where(rowi[:, 0:1] >= d, a, 0.0)
            ot = ot + a * pltpu.roll(vt, d, 0)
        o_tiles.append(ot)
    o = o + (jnp.concatenate(o_tiles, axis=0) if nt > 1 else o_tiles[0])

    if nt > 1:
        row = lax.broadcasted_iota(jnp.int32, (length, 1), 0)
        ri = lax.broadcasted_iota(jnp.int32, (length, length), 0)
        ci = lax.broadcasted_iota(jnp.int32, (length, length), 1)
        a_mat = jnp.zeros((length, length), F32)
        half = SUBLANES
        while half < length:
            blk = 2 * half
            ref_rows = [jnp.broadcast_to(b[b0 + half - 1:b0 + half, :], (blk, LANES))
                        for b0 in range(0, length, blk)]
            ref_b = jnp.concatenate(ref_rows, axis=0) if len(ref_rows) > 1 else ref_rows[0]
            e = jnp.exp(-jnp.abs(b - ref_b))
            right = (row % blk) >= half
            q_s = jnp.where(right, q * e, 0.0).astype(BF16)
            k_s = jnp.where(right, 0.0, k * e).astype(BF16)
            s = lax.dot_general(q_s, k_s, NT_DIMS, preferred_element_type=F32)
            if blk < length:
                s = jnp.where((ri // blk) == (ci // blk), s, 0.0)
            a_mat = a_mat + s
            half = blk
        o = o + jnp.dot(a_mat.astype(BF16), v.astype(BF16), preferred_element_type=F32)
    return o, st_new


def _hgrn_prompt_kernel(q_ref, k_ref, lf_ref, v_ref, o_ref, s_ref, st_ref, *, chunk, nchunks, nblocks):
    tb = pl.program_id(2)

    @pl.when(tb == 0)
    def _():
        st_ref[...] = jnp.zeros(st_ref.shape, F32)

    def body(ci, carry):
        rows = pl.ds(pl.multiple_of(ci * chunk, chunk), chunk)
        o, st_new = _hgrn_chunk(q_ref[rows, :], k_ref[rows, :], v_ref[rows, :], lf_ref[rows, :], st_ref[...])
        o_ref[rows, :] = o
        st_ref[...] = st_new
        return carry

    lax.fori_loop(0, nchunks, body, 0)

    @pl.when(tb == nblocks - 1)
    def _():
        s_ref[...] = st_ref[...].T


def _hgrn_prompt(q, k, lf, v, *, batch, seq, tblock, chunk):
    nblocks = seq // tblock
    tok = pl.BlockSpec((tblock, HG_DK), lambda b, h, t: (b * nblocks + t, h))
    return pl.pallas_call(
        functools.partial(_hgrn_prompt_kernel, chunk=chunk, nchunks=tblock // chunk, nblocks=nblocks),
        out_shape=(jax.ShapeDtypeStruct((batch * seq, D_MODEL), F32),
                   jax.ShapeDtypeStruct((batch, HG_HEADS, HG_DK, HG_DK), F32)),
        grid=(batch, HG_HEADS, nblocks),
        in_specs=[tok, tok, tok, tok],
        out_specs=(tok, pl.BlockSpec((None, None, HG_DK, HG_DK), lambda b, h, t: (b, h, 0, 0))),
        scratch_shapes=[pltpu.VMEM((HG_DK, HG_DK), F32)],
        compiler_params=pltpu.CompilerParams(
            dimension_semantics=("arbitrary", "arbitrary", "arbitrary"), vmem_limit_bytes=VMEM_LIMIT),
        name="hgrn_prompt_scan",
    )(q, k, lf, v)


def _hgrn_sample_kernel(q_ref, k_ref, lf_ref, v_ref, s0_ref, o_ref, s1_ref):
    def body(h, carry):
        o, st_new = _hgrn_chunk(q_ref[0, h], k_ref[0, h], v_ref[0, h], lf_ref[0, h], s0_ref[h].T)
        o_ref[0, h] = o
        s1_ref[h] = st_new.T
        return carry

    lax.fori_loop(0, HG_HEADS, body, 0)


def _hgrn_sample(q, k, lf, v, state):
    dec_batch = q.shape[0]
    tok = pl.BlockSpec((1, HG_HEADS, SUBLANES, HG_DK), lambda b: (b, 0, 0, 0))
    st = pl.BlockSpec((None, HG_HEADS, HG_DK, HG_DK), lambda b: (b, 0, 0, 0))
    return pl.pallas_call(
        _hgrn_sample_kernel,
        out_shape=(jax.ShapeDtypeStruct(q.shape, F32), jax.ShapeDtypeStruct(state.shape, F32)),
        grid=(dec_batch,),
        in_specs=[tok, tok, tok, tok, st],
        out_specs=(tok, st),
        compiler_params=pltpu.CompilerParams(
            dimension_semantics=("arbitrary",), vmem_limit_bytes=VMEM_LIMIT),
        name="hgrn_sample_step",
    )(q, k, lf, v, state)


def _rope_tables(pos):
    inv = jnp.power(ROPE_THETA, -jnp.arange(0, ROPE_DIM, 2, dtype=F32) / ROPE_DIM)
    ang = pos.astype(F32)[:, None] * inv[None, :]
    cos, sin = jnp.cos(ang), jnp.sin(ang)
    n = pos.shape[0]
    pad1 = jnp.ones((n, LANES - ROPE_DIM), F32)
    pad0 = jnp.zeros((n, LANES - ROPE_DIM), F32)
    zero_h = jnp.zeros_like(sin)
    cos_t = jnp.concatenate([cos, cos, pad1], axis=1)
    sa_t = jnp.concatenate([zero_h, sin, pad0], axis=1)
    sb_t = jnp.concatenate([-sin, zero_h, pad0], axis=1)
    return cos_t, sa_t, sb_t


def kernel(x_prompt, x_sample, cache_k, cache_v, state_hgrn, page_table, attn_norm, attn_w_in, attn_q_norm,
           attn_k_norm, attn_lambda, attn_subln, attn_w_out, hgrn_norm, hgrn_w_in, hgrn_lower_bounds,
           hgrn_out_norm, hgrn_w_out):
    batch, seq, _ = x_prompt.shape
    dec_batch, dec_seq, _ = x_sample.shape
    mp, ms = batch * seq, dec_batch * dec_seq
    xp = x_prompt.reshape(mp, D_MODEL)
    xs = x_sample.reshape(ms, D_MODEL)

    lq1, lk1, lq2, lk2 = attn_lambda[0].astype(F32)
    lam = (jnp.exp(jnp.sum(lq1 * lk1)) - jnp.exp(jnp.sum(lq2 * lk2)) + LAMBDA_INIT_0).reshape(1)
    w_in = attn_w_in[0].astype(BF16)
    w_out = attn_w_out[0].astype(BF16)
    norm_g = attn_norm[0].reshape(1, D_MODEL)
    subln_g = attn_subln[0].reshape(1, DA_V_DIM)

    tabs_p = _rope_tables(jnp.arange(seq))
    tabs_s = _rope_tables(jnp.tile(PAST_LEN + jnp.arange(dec_seq), dec_batch))

    q_p, k_p, kb_p, v_p, vb_p, g_p = _attn_proj(xp, norm_g, w_in, attn_q_norm[0], attn_k_norm[0], *tabs_p,
                                                tm=1024, tn=256)
    og_p = _flash_attention(lam, q_p, kb_p, vb_p, g_p, subln_g, batch=batch, seq=seq, tq=512)
    xp = _out_proj(og_p, w_out, xp, tm=256)

    q_s, k_s, _, v_s, _, g_s = _attn_proj(xs, norm_g, w_in, attn_q_norm[0], attn_k_norm[0], *tabs_s,
                                          tm=ms, tn=256)
    q5 = q_s.reshape(dec_batch, dec_seq, DA_HEADS, 2, DA_HEAD_DIM).transpose(0, 3, 2, 1, 4)
    zeros = jnp.zeros_like(q5[:, 0])
    qm = jnp.stack([jnp.concatenate([q5[:, 0], zeros], axis=-1),
                    jnp.concatenate([zeros, q5[:, 1]], axis=-1)], axis=1)
    qm = qm.reshape(dec_batch, SAMPLE_ROWS, DA_V_DIM)
    rows_new = dec_seq * DA_HEADS
    gate_r = g_s.reshape(dec_batch, dec_seq, DA_HEADS, DA_V_DIM).transpose(0, 2, 1, 3)
    gate_r = gate_r.reshape(dec_batch, rows_new, DA_V_DIM)
    og_s = _sample_attention(page_table, lam, qm, cache_k, cache_v,
                             k_s.reshape(dec_batch, rows_new, DA_V_DIM),
                             v_s.reshape(dec_batch, rows_new, DA_V_DIM), gate_r, subln_g, dec_seq=dec_seq)
    og_s = og_s.reshape(dec_batch, DA_HEADS, dec_seq, DA_V_DIM).transpose(0, 2, 1, 3).reshape(ms, D_MODEL)
    xs = _out_proj(og_s, w_out, xs, tm=ms)

    k_prompt = k_p.reshape(1, batch, seq, DA_HEADS, 2 * DA_HEAD_DIM)
    v_prompt = v_p.reshape(1, batch, seq, DA_HEADS, DA_V_DIM)
    k_sample = k_s.reshape(1, dec_batch, dec_seq, DA_HEADS, 2 * DA_HEAD_DIM)
    v_sample = v_s.reshape(1, dec_batch, dec_seq, DA_HEADS, DA_V_DIM)

    pr = jax.nn.softmax(hgrn_lower_bounds.astype(F32), axis=0)
    lb = (jnp.cumsum(pr, axis=0)[1] - pr[0]).reshape(1, D_MODEL)
    hw_in = hgrn_w_in[0].astype(BF16)
    hw_out = hgrn_w_out[0].astype(BF16)
    hnorm_g = hgrn_norm[0].reshape(1, D_MODEL)
    hout_g = hgrn_out_norm[0].reshape(1, D_MODEL)

    hq, hk, hlf, hv, hg = _hgrn_proj(xp, hnorm_g, hw_in, lb, tm=1024, tn=256)
    ho, s_prompt = _hgrn_prompt(hq, hk, hlf, hv, batch=batch, seq=seq, tblock=512, chunk=64)
    xp = _hgrn_out(ho, hg, hout_g, hw_out, xp, tm=256)

    sq, sk, slf, sv, sg = _hgrn_proj(xs, hnorm_g, hw_in, lb, tm=ms, tn=256)

    def to_heads(a):
        a = a.reshape(dec_batch, dec_seq, HG_HEADS, HG_DK).transpose(0, 2, 1, 3)
        return jnp.pad(a, ((0, 0), (0, 0), (0, SUBLANES - dec_seq), (0, 0)))

    so, s_sample = _hgrn_sample(to_heads(sq), to_heads(sk), to_heads(slf), to_heads(sv), state_hgrn[0])
    so = so[:, :, :dec_seq].transpose(0, 2, 1, 3).reshape(ms, D_MODEL)
    xs = _hgrn_out(so, sg, hout_g, hw_out, xs, tm=ms)

    return (xp.reshape(batch, seq, D_MODEL), xs.reshape(dec_batch, dec_seq, D_MODEL),
            k_prompt, v_prompt, k_sample, v_sample, s_prompt[None], s_sample[None])
```

```python
import functools
import math

import jax
import jax.numpy as jnp
from jax import lax
from jax.experimental import pallas as pl
from jax.experimental.pallas import tpu as pltpu

F32 = jnp.float32
BF16 = jnp.bfloat16

D_MODEL = 2048
PAST_LEN = 8192
PAGE_SIZE = 128
DA_HEADS = 8
DA_HEAD_DIM = 128
DA_V_DIM = 256
ROPE_DIM = 32
ROPE_THETA = 500000.0
HG_HEADS = 16
HG_DK = 128
EPS = 1e-6
LAMBDA_INIT_0 = 0.8 - 0.6 * math.exp(-0.3 * 0)

LANES = 128
SUBLANES = 8
VMEM_LIMIT = 56 * 1024 * 1024
NEG_BIG = -1e30
SM_SCALE = DA_HEAD_DIM ** -0.5
EXP2_SCALE = SM_SCALE * math.log2(math.e)

NT_DIMS = (((1,), (1,)), ((), ()))
TN_DIMS = (((0,), (0,)), ((), ()))


def _silu(x):
    return x * (1.0 / (1.0 + jnp.exp(-x)))


def _sigmoid(x):
    return 1.0 / (1.0 + jnp.exp(-x))


def _group_map(group, nblk):
    return lambda i, j: (i, jnp.clip(j - group * nblk, 0, nblk - 1))


def _norm_rows(x_ref, g_ref, xn_ref):
    x = x_ref[...]
    ms = jnp.mean(x * x, axis=-1, keepdims=True)
    xn_ref[...] = (x * lax.rsqrt(ms + EPS) * g_ref[...]).astype(BF16)


ROW_SPLITS = 4
MIN_SLAB_ROWS = 256


def _row_slabs(xn_ref, w_ref, emit):
    rows = min(xn_ref.shape[0], max(MIN_SLAB_ROWS, xn_ref.shape[0] // ROW_SPLITS))
    for r in range(xn_ref.shape[0] // rows):
        sl = slice(r * rows, (r + 1) * rows)
        emit(sl, jnp.dot(xn_ref[sl, :], w_ref[...], preferred_element_type=F32))


def _attn_proj_kernel(x_ref, g_ref, w_ref, qg_ref, kg_ref, cos_ref, sa_ref, sb_ref,
                      q_ref, k_ref, kb_ref, v_ref, vb_ref, gate_ref, xn_ref, *, tn, nblk):
    j = pl.program_id(1)

    @pl.when(j == 0)
    def _():
        _norm_rows(x_ref, g_ref, xn_ref)

    def qk_post(acc, sl, gain_ref, emit):
        for idx in range(tn // LANES):
            c = idx % 2
            a = acc[:, idx * LANES:(idx + 1) * LANES]
            ms = jnp.mean(a * a, axis=-1, keepdims=True)
            y = a * lax.rsqrt(ms + EPS) * gain_ref[c:c + 1, :]
            y = (y * cos_ref[sl, :] + pltpu.roll(y, ROPE_DIM // 2, 1) * sa_ref[sl, :]
                 + pltpu.roll(y, LANES - ROPE_DIM // 2, 1) * sb_ref[sl, :])
            emit(slice(idx * LANES, (idx + 1) * LANES), y)

    @pl.when(j < nblk)
    def _():
        def emit(sl, acc):
            def put(cols, y):
                q_ref[sl, cols] = y.astype(BF16)
            qk_post(acc, sl, qg_ref, put)
        _row_slabs(xn_ref, w_ref, emit)

    @pl.when((j >= nblk) & (j < 2 * nblk))
    def _():
        def emit(sl, acc):
            def put(cols, y):
                k_ref[sl, cols] = y
                kb_ref[sl, cols] = y.astype(BF16)
            qk_post(acc, sl, kg_ref, put)
        _row_slabs(xn_ref, w_ref, emit)

    @pl.when((j >= 2 * nblk) & (j < 3 * nblk))
    def _():
        def emit(sl, acc):
            v_ref[sl, :] = acc
            vb_ref[sl, :] = acc.astype(BF16)
        _row_slabs(xn_ref, w_ref, emit)

    @pl.when(j >= 3 * nblk)
    def _():
        def emit(sl, acc):
            gate_ref[sl, :] = acc
        _row_slabs(xn_ref, w_ref, emit)


def _attn_proj(x, norm_g, w_bf, q_g, k_g, cos_t, sa_t, sb_t, *, tm, tn):
    m = x.shape[0]
    nblk = D_MODEL // tn
    tab_blocks = cos_t.shape[0] // tm
    tab_spec = pl.BlockSpec((tm, LANES), lambda i, j: (i % tab_blocks, 0))
    small = lambda shape: pl.BlockSpec(shape, lambda i, j: (0, 0))
    out_sd = lambda dt: jax.ShapeDtypeStruct((m, D_MODEL), dt)
    out_spec = lambda g: pl.BlockSpec((tm, tn), _group_map(g, nblk))
    return pl.pallas_call(
        functools.partial(_attn_proj_kernel, tn=tn, nblk=nblk),
        out_shape=(out_sd(BF16), out_sd(F32), out_sd(BF16), out_sd(F32), out_sd(BF16), out_sd(F32)),
        grid=(m // tm, 4 * nblk),
        in_specs=[pl.BlockSpec((tm, D_MODEL), lambda i, j: (i, 0)),
                  small((1, D_MODEL)),
                  pl.BlockSpec((D_MODEL, tn), lambda i, j: (0, j)),
                  small((2, LANES)), small((2, LANES)),
                  tab_spec, tab_spec, tab_spec],
        out_specs=(out_spec(0), out_spec(1), out_spec(1), out_spec(2), out_spec(2), out_spec(3)),
        scratch_shapes=[pltpu.VMEM((tm, D_MODEL), BF16)],
        compiler_params=pltpu.CompilerParams(
            dimension_semantics=("arbitrary", "arbitrary"), vmem_limit_bytes=VMEM_LIMIT),
        name="attn_proj",
    )(x, norm_g, w_bf, q_g, k_g, cos_t, sa_t, sb_t)


def _hgrn_proj_kernel(x_ref, g_ref, w_ref, lb_ref, q_ref, k_ref, lf_ref, i_ref, gate_ref, xn_ref, *, nblk):
    j = pl.program_id(1)

    @pl.when(j == 0)
    def _():
        _norm_rows(x_ref, g_ref, xn_ref)

    @pl.when(j < nblk)
    def _():
        def emit(sl, acc):
            q_ref[sl, :] = _silu(acc)
        _row_slabs(xn_ref, w_ref, emit)

    @pl.when((j >= nblk) & (j < 2 * nblk))
    def _():
        def emit(sl, acc):
            lb = lb_ref[...]
            fg = lb + (1.0 - lb) * _sigmoid(acc)
            k_ref[sl, :] = 1.0 - fg
            lf_ref[sl, :] = jnp.log(fg)
        _row_slabs(xn_ref, w_ref, emit)

    @pl.when((j >= 2 * nblk) & (j < 3 * nblk))
    def _():
        def emit(sl, acc):
            i_ref[sl, :] = acc
        _row_slabs(xn_ref, w_ref, emit)

    @pl.when(j >= 3 * nblk)
    def _():
        def emit(sl, acc):
            gate_ref[sl, :] = acc
        _row_slabs(xn_ref, w_ref, emit)


def _hgrn_proj(x, norm_g, w_bf, lb, *, tm, tn):
    m = x.shape[0]
    nblk = D_MODEL // tn
    out_sd = jax.ShapeDtypeStruct((m, D_MODEL), F32)
    out_spec = lambda g: pl.BlockSpec((tm, tn), _group_map(g, nblk))
    return pl.pallas_call(
        functools.partial(_hgrn_proj_kernel, nblk=nblk),
        out_shape=(out_sd,) * 5,
        grid=(m // tm, 4 * nblk),
        in_specs=[pl.BlockSpec((tm, D_MODEL), lambda i, j: (i, 0)),
                  pl.BlockSpec((1, D_MODEL), lambda i, j: (0, 0)),
                  pl.BlockSpec((D_MODEL, tn), lambda i, j: (0, j)),
                  pl.BlockSpec((1, tn), lambda i, j: (0, jnp.clip(j - nblk, 0, nblk - 1)))],
        out_specs=(out_spec(0), out_spec(1), out_spec(1), out_spec(2), out_spec(3)),
        scratch_shapes=[pltpu.VMEM((tm, D_MODEL), BF16)],
        compiler_params=pltpu.CompilerParams(
            dimension_semantics=("arbitrary", "arbitrary"), vmem_limit_bytes=VMEM_LIMIT),
        name="hgrn_proj",
    )(x, norm_g, w_bf, lb)


def _out_proj_kernel(a_ref, w_ref, x_ref, y_ref):
    y_ref[...] = x_ref[...] + jnp.dot(a_ref[...], w_ref[...], preferred_element_type=F32)


def _out_proj(a_bf, w_bf, x, *, tm):
    m = x.shape[0]
    row = pl.BlockSpec((tm, D_MODEL), lambda i: (i, 0))
    return pl.pallas_call(
        _out_proj_kernel,
        out_shape=jax.ShapeDtypeStruct((m, D_MODEL), F32),
        grid=(m // tm,),
        in_specs=[row, pl.BlockSpec((D_MODEL, D_MODEL), lambda i: (0, 0)), row],
        out_specs=row,
        compiler_params=pltpu.CompilerParams(
            dimension_semantics=("arbitrary",), vmem_limit_bytes=VMEM_LIMIT),
        name="attn_out_proj",
    )(a_bf, w_bf, x)


def _hgrn_out_kernel(o_ref, gate_ref, g_ref, w_ref, x_ref, y_ref):
    o = o_ref[...]
    ms = jnp.mean(o * o, axis=-1, keepdims=True)
    a = (o * lax.rsqrt(ms + EPS) * g_ref[...]) * _silu(gate_ref[...])
    y_ref[...] = x_ref[...] + jnp.dot(a.astype(BF16), w_ref[...], preferred_element_type=F32)


def _hgrn_out(o, gate, out_g, w_bf, x, *, tm):
    m = x.shape[0]
    row = pl.BlockSpec((tm, D_MODEL), lambda i: (i, 0))
    return pl.pallas_call(
        _hgrn_out_kernel,
        out_shape=jax.ShapeDtypeStruct((m, D_MODEL), F32),
        grid=(m // tm,),
        in_specs=[row, row, pl.BlockSpec((1, D_MODEL), lambda i: (0, 0)),
                  pl.BlockSpec((D_MODEL, D_MODEL), lambda i: (0, 0)), row],
        out_specs=row,
        compiler_params=pltpu.CompilerParams(
            dimension_semantics=("arbitrary",), vmem_limit_bytes=VMEM_LIMIT),
        name="hgrn_out_proj",
    )(o, gate, out_g, w_bf, x)


def _lane_tiles(x):
    return [x[:, j * LANES:(j + 1) * LANES] for j in range(x.shape[1] // LANES)]


def _subln_gate(o, gate, sg_ref):
    ms = jnp.mean(o * o, axis=-1, keepdims=True)
    o = (o * lax.rsqrt(ms + EPS) * sg_ref[...]) * (1.0 - LAMBDA_INIT_0)
    return o * _silu(gate)


def _flash_kernel(lam_ref, q_ref, k_ref, v_ref, gate_ref, sg_ref, o_ref, m0, m1, l0, l1, a0, a1, *, tq):
    m_refs, l_refs, acc_refs = (m0, m1), (l0, l1), (a0, a1)
    qi = pl.program_id(2)
    for c in range(2):
        m_refs[c][...] = jnp.full(m_refs[c].shape, NEG_BIG, F32)
        l_refs[c][...] = jnp.zeros(l_refs[c].shape, F32)
        acc_refs[c][...] = jnp.zeros(acc_refs[c].shape, F32)

    def step(ki, masked):
        kv_rows = pl.ds(pl.multiple_of(ki * tq, tq), tq)
        v_bf = v_ref[kv_rows, :]
        comps = [slice(c * LANES, (c + 1) * LANES) for c in range(2)]
        scores = [lax.dot_general(q_ref[:, comp], k_ref[kv_rows, comp], NT_DIMS, preferred_element_type=F32)
                  for comp in comps]
        probs, alphas = [], []
        for c in range(2):
            s = scores[c]
            if masked:
                r = lax.broadcasted_iota(jnp.int32, (tq, tq), 0)
                col = lax.broadcasted_iota(jnp.int32, (tq, tq), 1)
                s = jnp.where(col <= r, s, NEG_BIG)
            tiles = _lane_tiles(s)
            m_prev = m_refs[c][...]
            m_new = jnp.maximum(m_prev, jnp.max(functools.reduce(jnp.maximum, tiles), axis=-1, keepdims=True))
            alpha = jnp.exp2((m_prev - m_new) * EXP2_SCALE)
            p_tiles = [jnp.exp2((t - m_new) * EXP2_SCALE) for t in tiles]
            l_refs[c][...] = alpha * l_refs[c][...] + functools.reduce(jnp.add, p_tiles)
            m_refs[c][...] = m_new
            probs.append(jnp.concatenate(p_tiles, axis=-1).astype(BF16))
            alphas.append(alpha)
        pv = jnp.dot(jnp.concatenate(probs, axis=0), v_bf, preferred_element_type=F32)
        for c in range(2):
            acc_refs[c][...] = (jnp.concatenate([alphas[c], alphas[c]], axis=-1) * acc_refs[c][...]
                                + pv[c * tq:(c + 1) * tq, :])

    def body(ki, carry):
        step(ki, False)
        return carry

    lax.fori_loop(0, qi, body, 0)
    step(qi, True)

    norm = [acc_refs[c][...] / jnp.sum(l_refs[c][...], axis=-1, keepdims=True) for c in range(2)]
    o = norm[0] - lam_ref[0] * norm[1]
    o_ref[...] = _subln_gate(o, gate_ref[...], sg_ref).astype(BF16)


def _flash_attention(lam, q_bf, k_bf, v_bf, gate, subln_g, *, batch, seq, tq):
    nq = seq // tq
    qmap = lambda b, h, qi: (b * nq + qi, h)
    kvmap = lambda b, h, qi: (b, h)
    stat = pltpu.VMEM((tq, LANES), F32)
    acc = pltpu.VMEM((tq, DA_V_DIM), F32)
    return pl.pallas_call(
        functools.partial(_flash_kernel, tq=tq),
        out_shape=jax.ShapeDtypeStruct((batch * seq, DA_HEADS * DA_V_DIM), BF16),
        grid=(batch, DA_HEADS, nq),
        in_specs=[pl.BlockSpec(memory_space=pltpu.SMEM),
                  pl.BlockSpec((tq, DA_V_DIM), qmap),
                  pl.BlockSpec((seq, DA_V_DIM), kvmap),
                  pl.BlockSpec((seq, DA_V_DIM), kvmap),
                  pl.BlockSpec((tq, DA_V_DIM), qmap),
                  pl.BlockSpec((1, DA_V_DIM), lambda b, h, qi: (0, 0))],
        out_specs=pl.BlockSpec((tq, DA_V_DIM), qmap),
        scratch_shapes=[stat, stat, stat, stat, acc, acc],
        compiler_params=pltpu.CompilerParams(
            dimension_semantics=("arbitrary", "arbitrary", "arbitrary"), vmem_limit_bytes=VMEM_LIMIT),
        name="flash_diff_attn",
    )(lam, q_bf, k_bf, v_bf, gate, subln_g)


SAMPLE_ROWS = 64
PAGE_ROWS = PAGE_SIZE * DA_HEADS
PAGES_PER_STEP = 4


def _sample_attn_kernel(pt_ref, lam_ref, qm_ref, *rest, n_steps):
    g = PAGES_PER_STEP
    k_refs, v_refs = rest[:g], rest[g:2 * g]
    bias_ref, kn_ref, vn_ref, biasn_ref, gate_ref, sg_ref, o_ref, m_ref, l_ref, acc_ref = rest[2 * g:]
    step = pl.program_id(1)

    @pl.when(step == 0)
    def _():
        m_ref[...] = jnp.full(m_ref.shape, NEG_BIG, F32)
        l_ref[...] = jnp.zeros(l_ref.shape, F32)
        acc_ref[...] = jnp.zeros(acc_ref.shape, F32)

    def scores(kf, bias):
        s = lax.dot_general(qm_ref[0], kf.astype(BF16), NT_DIMS, preferred_element_type=F32)
        return s * SM_SCALE + bias

    flat = lambda ref: ref[0, 0].reshape(PAGE_ROWS, DA_V_DIM)
    parts = []
    for i in range(g):
        tiles = _lane_tiles(scores(flat(k_refs[i]), bias_ref[...]))
        m_i = jnp.broadcast_to(jnp.max(functools.reduce(jnp.maximum, tiles), axis=-1, keepdims=True),
                               (SAMPLE_ROWS, LANES))
        p_tiles = [jnp.exp(t - m_i) for t in tiles]
        pv_i = jnp.dot(jnp.concatenate(p_tiles, axis=-1).astype(BF16), flat(v_refs[i]).astype(BF16),
                       preferred_element_type=F32)
        parts.append((m_i, functools.reduce(jnp.add, p_tiles), pv_i))
    m_prev = m_ref[...]
    m_new = functools.reduce(jnp.maximum, [m_prev] + [part[0] for part in parts])
    alpha = jnp.exp(m_prev - m_new)
    l_new = alpha * l_ref[...]
    acc_new = jnp.concatenate([alpha, alpha], axis=-1) * acc_ref[...]
    for m_i, l_i, pv_i in parts:
        w = jnp.exp(m_i - m_new)
        l_new = l_new + w * l_i
        acc_new = acc_new + jnp.concatenate([w, w], axis=-1) * pv_i
    l_ref[...] = l_new
    acc_ref[...] = acc_new
    m_ref[...] = m_new

    @pl.when(step == n_steps - 1)
    def _():
        m_old = m_ref[:, 0:1]
        l_old = jnp.sum(l_ref[...], axis=-1, keepdims=True)
        s = scores(kn_ref[0], biasn_ref[...])
        m_fin = jnp.maximum(m_old, jnp.max(s, axis=-1, keepdims=True))
        a_fin = jnp.exp(m_old - m_fin)
        p = jnp.exp(s - m_fin)
        l_fin = a_fin * l_old + jnp.sum(p, axis=-1, keepdims=True)
        acc = a_fin * acc_ref[...] + jnp.dot(p.astype(BF16), vn_ref[0].astype(BF16), preferred_element_type=F32)
        norm = acc / l_fin
        half = SAMPLE_ROWS // 2
        o = norm[0:half, :] - lam_ref[0] * norm[half:, :]
        o_ref[0] = _subln_gate(o, gate_ref[0], sg_ref).astype(BF16)


def _sample_attention(page_table, lam, qm, cache_k, cache_v, k_new, v_new, gate_r, subln_g, *, dec_seq):
    dec_batch, n_pages = page_table.shape
    g = PAGES_PER_STEP
    n_steps = n_pages // g
    rows_new = dec_seq * DA_HEADS
    row_head = (jnp.arange(SAMPLE_ROWS) // dec_seq) % DA_HEADS
    col = jnp.arange(PAGE_ROWS)
    bias = jnp.where((col % DA_HEADS)[None, :] == row_head[:, None], 0.0, NEG_BIG).astype(F32)
    col_n = jnp.arange(rows_new)
    valid_n = ((col_n % DA_HEADS)[None, :] == row_head[:, None]) & (
        (col_n // DA_HEADS)[None, :] <= (jnp.arange(SAMPLE_ROWS) % dec_seq)[:, None])
    bias_n = jnp.where(valid_n, 0.0, NEG_BIG).astype(F32)

    def page_spec(i):
        return pl.BlockSpec((1, 1, PAGE_SIZE, DA_HEADS, DA_V_DIM),
                            lambda b, p, pt: (0, pt[b, p * g + i], 0, 0, 0))

    per_b = lambda rows, cols: pl.BlockSpec((1, rows, cols), lambda b, p, pt: (b, 0, 0))
    const = lambda shape: pl.BlockSpec(shape, lambda b, p, pt: (0, 0))
    grid_spec = pltpu.PrefetchScalarGridSpec(
        num_scalar_prefetch=1,
        grid=(dec_batch, n_steps),
        in_specs=[pl.BlockSpec(memory_space=pltpu.SMEM), per_b(SAMPLE_ROWS, DA_V_DIM)]
        + [page_spec(i) for i in range(g)] + [page_spec(i) for i in range(g)]
        + [const((SAMPLE_ROWS, PAGE_ROWS)), per_b(rows_new, DA_V_DIM), per_b(rows_new, DA_V_DIM),
           const((SAMPLE_ROWS, rows_new)), per_b(rows_new, DA_V_DIM), const((1, DA_V_DIM))],
        out_specs=per_b(rows_new, DA_V_DIM),
        scratch_shapes=[pltpu.VMEM((SAMPLE_ROWS, LANES), F32), pltpu.VMEM((SAMPLE_ROWS, LANES), F32),
                        pltpu.VMEM((SAMPLE_ROWS, DA_V_DIM), F32)],
    )
    return pl.pallas_call(
        functools.partial(_sample_attn_kernel, n_steps=n_steps),
        out_shape=jax.ShapeDtypeStruct((dec_batch, rows_new, DA_V_DIM), BF16),
        grid_spec=grid_spec,
        compiler_params=pltpu.CompilerParams(
            dimension_semantics=("arbitrary", "arbitrary"), vmem_limit_bytes=VMEM_LIMIT),
        name="paged_diff_attn",
    )(page_table, lam, qm, *([cache_k] * g), *([cache_v] * g), bias, k_new, v_new, bias_n, gate_r, subln_g)


def _tile_cumsum(x, rowi):
    for s in (1, 2, 4):
        x = x + jnp.where(rowi >= s, pltpu.roll(x, s, 0), 0.0)
    return x


def _hgrn_chunk(q, k, v, lf, st):
    length = q.shape[0]
    nt = length // SUBLANES
    rowi = lax.broadcasted_iota(jnp.int32, (SUBLANES, LANES), 0)
    tile = lambda a, r: a[r * SUBLANES:(r + 1) * SUBLANES, :]

    b_tiles = []
    carry = None
    for r in range(nt):
        c = _tile_cumsum(tile(lf, r), rowi)
        if carry is not None:
            c = c + carry
        carry = c[SUBLANES - 1:SUBLANES, :]
        b_tiles.append(c)
    b = jnp.concatenate(b_tiles, axis=0) if nt > 1 else b_tiles[0]
    b_last = carry

    qe = (q * jnp.exp(b)).astype(BF16)
    o = lax.dot_general(qe, st.astype(BF16), NT_DIMS, preferred_element_type=F32)
    kd = (k * jnp.exp(b_last - b)).astype(BF16)
    st_new = st * jnp.exp(b_last) + lax.dot_general(v.astype(BF16), kd, TN_DIMS, preferred_element_type=F32)

    o_tiles = []
    for r in range(nt):
        bt, qt, kt, vt = b_tiles[r], tile(q, r), tile(k, r), tile(v, r)
        ot = jnp.sum(qt * kt, axis=-1, keepdims=True) * vt
        for d in range(1, SUBLANES):
            dec = jnp.exp(jnp.minimum(bt - pltpu.roll(bt, d, 0), 0.0))
            a = jnp.sum(qt * pltpu.roll(kt, d, 0) * dec, axis=-1, keepdims=True)
            a = jnp.where(rowi[:, 0:1] >= d, a, 0.0)
            ot = ot + a * pltpu.roll(vt, d, 0)
        o_tiles.append(ot)
    o = o + (jnp.concatenate(o_tiles, axis=0) if nt > 1 else o_tiles[0])

    if nt > 1:
        row = lax.broadcasted_iota(jnp.int32, (length, 1), 0)
        ri = lax.broadcasted_iota(jnp.int32, (length, length), 0)
        ci = lax.broadcasted_iota(jnp.int32, (length, length), 1)
        a_mat = jnp.zeros((length, length), F32)
        half = SUBLANES
        while half < length:
            blk = 2 * half
            ref_rows = [jnp.broadcast_to(b[b0 + half - 1:b0 + half, :], (blk, LANES))
                        for b0 in range(0, length, blk)]
            ref_b = jnp.concatenate(ref_rows, axis=0) if len(ref_rows) > 1 else ref_rows[0]
            e = jnp.exp(-jnp.abs(b - ref_b))
            right = (row % blk) >= half
            q_s = jnp.where(right, q * e, 0.0).astype(BF16)
            k_s = jnp.where(right, 0.0, k * e).astype(BF16)
            s = lax.dot_general(q_s, k_s, NT_DIMS, preferred_element_type=F32)
            if blk < length:
                s = jnp.where((ri // blk) == (ci // blk), s, 0.0)
            a_mat = a_mat + s
            half = blk
        o = o + jnp.dot(a_mat.astype(BF16), v.astype(BF16), preferred_element_type=F32)
    return o, st_new


HEADS_PER_STEP = 4


def _hgrn_prompt_kernel(q_ref, k_ref, lf_ref, v_ref, o_ref, s_ref, st_ref, *, chunk, nchunks, nblocks):
    tb = pl.program_id(2)

    @pl.when(tb == 0)
    def _():
        st_ref[...] = jnp.zeros(st_ref.shape, F32)

    def body(ci, carry):
        rows = pl.ds(pl.multiple_of(ci * chunk, chunk), chunk)
        for hh in range(HEADS_PER_STEP):
            cols = slice(hh * HG_DK, (hh + 1) * HG_DK)
            o, st_new = _hgrn_chunk(q_ref[rows, cols], k_ref[rows, cols], v_ref[rows, cols], lf_ref[rows, cols],
                                    st_ref[hh])
            o_ref[rows, cols] = o
            st_ref[hh] = st_new
        return carry

    lax.fori_loop(0, nchunks, body, 0)

    @pl.when(tb == nblocks - 1)
    def _():
        for hh in range(HEADS_PER_STEP):
            s_ref[hh] = st_ref[hh].T


def _hgrn_prompt(q, k, lf, v, *, batch, seq, tblock, chunk):
    nblocks = seq // tblock
    width = HEADS_PER_STEP * HG_DK
    tok = pl.BlockSpec((tblock, width), lambda b, h, t: (b * nblocks + t, h))
    return pl.pallas_call(
        functools.partial(_hgrn_prompt_kernel, chunk=chunk, nchunks=tblock // chunk, nblocks=nblocks),
        out_shape=(jax.ShapeDtypeStruct((batch * seq, D_MODEL), F32),
                   jax.ShapeDtypeStruct((batch, HG_HEADS, HG_DK, HG_DK), F32)),
        grid=(batch, HG_HEADS // HEADS_PER_STEP, nblocks),
        in_specs=[tok, tok, tok, tok],
        out_specs=(tok, pl.BlockSpec((None, HEADS_PER_STEP, HG_DK, HG_DK), lambda b, h, t: (b, h, 0, 0))),
        scratch_shapes=[pltpu.VMEM((HEADS_PER_STEP, HG_DK, HG_DK), F32)],
        compiler_params=pltpu.CompilerParams(
            dimension_semantics=("arbitrary", "arbitrary", "arbitrary"), vmem_limit_bytes=VMEM_LIMIT),
        name="hgrn_prompt_scan",
    )(q, k, lf, v)


def _hgrn_sample_kernel(q_ref, k_ref, lf_ref, v_ref, s0_ref, o_ref, s1_ref):
    def body(hb, carry):
        for hh in range(HEADS_PER_STEP):
            h = hb * HEADS_PER_STEP + hh
            o, st_new = _hgrn_chunk(q_ref[0, h], k_ref[0, h], v_ref[0, h], lf_ref[0, h], s0_ref[h].T)
            o_ref[0, h] = o
            s1_ref[h] = st_new.T
        return carry

    lax.fori_loop(0, HG_HEADS // HEADS_PER_STEP, body, 0)


def _hgrn_sample(q, k, lf, v, state):
    dec_batch = q.shape[0]
    tok = pl.BlockSpec((1, HG_HEADS, SUBLANES, HG_DK), lambda b: (b, 0, 0, 0))
    st = pl.BlockSpec((None, HG_HEADS, HG_DK, HG_DK), lambda b: (b, 0, 0, 0))
    return pl.pallas_call(
        _hgrn_sample_kernel,
        out_shape=(jax.ShapeDtypeStruct(q.shape, F32), jax.ShapeDtypeStruct(state.shape, F32)),
        grid=(dec_batch,),
        in_specs=[tok, tok, tok, tok, st],
        out_specs=(tok, st),
        compiler_params=pltpu.CompilerParams(
            dimension_semantics=("arbitrary",), vmem_limit_bytes=VMEM_LIMIT),
        name="hgrn_sample_step",
    )(q, k, lf, v, state)


def _rope_tables(pos):
    inv = jnp.power(ROPE_THETA, -jnp.arange(0, ROPE_DIM, 2, dtype=F32) / ROPE_DIM)
    ang = pos.astype(F32)[:, None] * inv[None, :]
    cos, sin = jnp.cos(ang), jnp.sin(ang)
    n = pos.shape[0]
    pad1 = jnp.ones((n, LANES - ROPE_DIM), F32)
    pad0 = jnp.zeros((n, LANES - ROPE_DIM), F32)
    zero_h = jnp.zeros_like(sin)
    cos_t = jnp.concatenate([cos, cos, pad1], axis=1)
    sa_t = jnp.concatenate([zero_h, sin, pad0], axis=1)
    sb_t = jnp.concatenate([-sin, zero_h, pad0], axis=1)
    return cos_t, sa_t, sb_t


def kernel(x_prompt, x_sample, cache_k, cache_v, state_hgrn, page_table, attn_norm, attn_w_in, attn_q_norm,
           attn_k_norm, attn_lambda, attn_subln, attn_w_out, hgrn_norm, hgrn_w_in, hgrn_lower_bounds,
           hgrn_out_norm, hgrn_w_out):
    batch, seq, _ = x_prompt.shape
    dec_batch, dec_seq, _ = x_sample.shape
    mp, ms = batch * seq, dec_batch * dec_seq
    xp = x_prompt.reshape(mp, D_MODEL)
    xs = x_sample.reshape(ms, D_MODEL)

    lq1, lk1, lq2, lk2 = attn_lambda[0].astype(F32)
    lam = (jnp.exp(jnp.sum(lq1 * lk1)) - jnp.exp(jnp.sum(lq2 * lk2)) + LAMBDA_INIT_0).reshape(1)
    w_in = attn_w_in[0].astype(BF16)
    w_out = attn_w_out[0].astype(BF16)
    norm_g = attn_norm[0].reshape(1, D_MODEL)
    subln_g = attn_subln[0].reshape(1, DA_V_DIM)

    tabs_p = _rope_tables(jnp.arange(seq))
    tabs_s = _rope_tables(jnp.tile(PAST_LEN + jnp.arange(dec_seq), dec_batch))

    q_p, k_p, kb_p, v_p, vb_p, g_p = _attn_proj(xp, norm_g, w_in, attn_q_norm[0], attn_k_norm[0], *tabs_p,
                                                tm=1024, tn=512)
    og_p = _flash_attention(lam, q_p, kb_p, vb_p, g_p, subln_g, batch=batch, seq=seq, tq=512)
    xp = _out_proj(og_p, w_out, xp, tm=256)

    q_s, k_s, _, v_s, _, g_s = _attn_proj(xs, norm_g, w_in, attn_q_norm[0], attn_k_norm[0], *tabs_s,
                                          tm=ms, tn=1024)
    q5 = q_s.reshape(dec_batch, dec_seq, DA_HEADS, 2, DA_HEAD_DIM).transpose(0, 3, 2, 1, 4)
    zeros = jnp.zeros_like(q5[:, 0])
    qm = jnp.stack([jnp.concatenate([q5[:, 0], zeros], axis=-1),
                    jnp.concatenate([zeros, q5[:, 1]], axis=-1)], axis=1)
    qm = qm.reshape(dec_batch, SAMPLE_ROWS, DA_V_DIM)
    rows_new = dec_seq * DA_HEADS
    gate_r = g_s.reshape(dec_batch, dec_seq, DA_HEADS, DA_V_DIM).transpose(0, 2, 1, 3)
    gate_r = gate_r.reshape(dec_batch, rows_new, DA_V_DIM)
    og_s = _sample_attention(page_table, lam, qm, cache_k, cache_v,
                             k_s.reshape(dec_batch, rows_new, DA_V_DIM),
                             v_s.reshape(dec_batch, rows_new, DA_V_DIM), gate_r, subln_g, dec_seq=dec_seq)
    og_s = og_s.reshape(dec_batch, DA_HEADS, dec_seq, DA_V_DIM).transpose(0, 2, 1, 3).reshape(ms, D_MODEL)
    xs = _out_proj(og_s, w_out, xs, tm=ms)

    k_prompt = k_p.reshape(1, batch, seq, DA_HEADS, 2 * DA_HEAD_DIM)
    v_prompt = v_p.reshape(1, batch, seq, DA_HEADS, DA_V_DIM)
    k_sample = k_s.reshape(1, dec_batch, dec_seq, DA_HEADS, 2 * DA_HEAD_DIM)
    v_sample = v_s.reshape(1, dec_batch, dec_seq, DA_HEADS, DA_V_DIM)

    pr = jax.nn.softmax(hgrn_lower_bounds.astype(F32), axis=0)
    lb = (jnp.cumsum(pr, axis=0)[1] - pr[0]).reshape(1, D_MODEL)
    hw_in = hgrn_w_in[0].astype(BF16)
    hw_out = hgrn_w_out[0].astype(BF16)
    hnorm_g = hgrn_norm[0].reshape(1, D_MODEL)
    hout_g = hgrn_out_norm[0].reshape(1, D_MODEL)

    hq, hk, hlf, hv, hg = _hgrn_proj(xp, hnorm_g, hw_in, lb, tm=1024, tn=512)
    ho, s_prompt = _hgrn_prompt(hq, hk, hlf, hv, batch=batch, seq=seq, tblock=512, chunk=64)
    xp = _hgrn_out(ho, hg, hout_g, hw_out, xp, tm=256)

    sq, sk, slf, sv, sg = _hgrn_proj(xs, hnorm_g, hw_in, lb, tm=ms, tn=1024)

    def to_heads(a):
        a = a.reshape(dec_batch, dec_seq, HG_HEADS, HG_DK).transpose(0, 2, 1, 3)
        return jnp.pad(a, ((0, 0), (0, 0), (0, SUBLANES - dec_seq), (0, 0)))

    so, s_sample = _hgrn_sample(to_heads(sq), to_heads(sk), to_heads(slf), to_heads(sv), state_hgrn[0])
    so = so[:, :, :dec_seq].transpose(0, 2, 1, 3).reshape(ms, D_MODEL)
    xs = _hgrn_out(so, sg, hout_g, hw_out, xs, tm=ms)

    return (xp.reshape(batch, seq, D_MODEL), xs.reshape(dec_batch, dec_seq, D_MODEL),
            k_prompt, v_prompt, k_sample, v_sample, s_prompt[None], s_sample[None])
```

```python
import functools
import math

import jax
import jax.numpy as jnp
from jax import lax
from jax.experimental import pallas as pl
from jax.experimental.pallas import tpu as pltpu

F32 = jnp.float32
BF16 = jnp.bfloat16

D_MODEL = 2048
PAST_LEN = 8192
PAGE_SIZE = 128
DA_HEADS = 8
DA_HEAD_DIM = 128
DA_V_DIM = 256
ROPE_DIM = 32
ROPE_THETA = 500000.0
HG_HEADS = 16
HG_DK = 128
EPS = 1e-6
LAMBDA_INIT_0 = 0.8 - 0.6 * math.exp(-0.3 * 0)

LANES = 128
SUBLANES = 8
VMEM_LIMIT = 56 * 1024 * 1024
NEG_BIG = -1e30
LOG2E = math.log2(math.e)
EXP2_SCALE = DA_HEAD_DIM ** -0.5 * LOG2E

NT_DIMS = (((1,), (1,)), ((), ()))
TN_DIMS = (((0,), (0,)), ((), ()))


def _silu(x):
    return x * (1.0 / (1.0 + jnp.exp(-x)))


def _sigmoid(x):
    return 1.0 / (1.0 + jnp.exp(-x))


def _group_map(group, nblk):
    return lambda i, j: (i, jnp.clip(j - group * nblk, 0, nblk - 1))


def _norm_rows(x_ref, g_ref, xn_ref):
    x = x_ref[...]
    ms = jnp.mean(x * x, axis=-1, keepdims=True)
    xn_ref[...] = (x * lax.rsqrt(ms + EPS) * g_ref[...]).astype(BF16)


ROW_SPLITS = 4
MIN_SLAB_ROWS = 256


def _row_slabs(xn_ref, w_ref, emit):
    rows = min(xn_ref.shape[0], max(MIN_SLAB_ROWS, xn_ref.shape[0] // ROW_SPLITS))
    for r in range(xn_ref.shape[0] // rows):
        sl = slice(r * rows, (r + 1) * rows)
        emit(sl, jnp.dot(xn_ref[sl, :], w_ref[...], preferred_element_type=F32))


def _bf16_weights(w_ref, wb_ref):
    if wb_ref is None:
        return w_ref
    wb_ref[...] = w_ref[...].astype(BF16)
    return wb_ref


def _attn_proj_kernel(x_ref, g_ref, w_ref, qg_ref, kg_ref, cos_ref, sa_ref, sb_ref,
                      q_ref, k_ref, kb_ref, v_ref, vb_ref, gate_ref, *rest, tn, nblk):
    wb_ref, xn_ref = rest if len(rest) == 2 else (None, rest[0])
    w_ref = _bf16_weights(w_ref, wb_ref)
    j = pl.program_id(1)

    @pl.when(j == 0)
    def _():
        _norm_rows(x_ref, g_ref, xn_ref)

    def qk_post(acc, sl, gain_ref, emit):
        for idx in range(tn // LANES):
            c = idx % 2
            a = acc[:, idx * LANES:(idx + 1) * LANES]
            ms = jnp.mean(a * a, axis=-1, keepdims=True)
            y = a * lax.rsqrt(ms + EPS) * gain_ref[c:c + 1, :]
            y = (y * cos_ref[sl, :] + pltpu.roll(y, ROPE_DIM // 2, 1) * sa_ref[sl, :]
                 + pltpu.roll(y, LANES - ROPE_DIM // 2, 1) * sb_ref[sl, :])
            emit(slice(idx * LANES, (idx + 1) * LANES), y)

    @pl.when(j < nblk)
    def _():
        def emit(sl, acc):
            def put(cols, y):
                q_ref[sl, cols] = (y * EXP2_SCALE).astype(BF16)
            qk_post(acc, sl, qg_ref, put)
        _row_slabs(xn_ref, w_ref, emit)

    @pl.when((j >= nblk) & (j < 2 * nblk))
    def _():
        def emit(sl, acc):
            def put(cols, y):
                k_ref[sl, cols] = y
                kb_ref[sl, cols] = y.astype(BF16)
            qk_post(acc, sl, kg_ref, put)
        _row_slabs(xn_ref, w_ref, emit)

    @pl.when((j >= 2 * nblk) & (j < 3 * nblk))
    def _():
        def emit(sl, acc):
            v_ref[sl, :] = acc
            vb_ref[sl, :] = acc.astype(BF16)
        _row_slabs(xn_ref, w_ref, emit)

    @pl.when(j >= 3 * nblk)
    def _():
        def emit(sl, acc):
            gate_ref[sl, :] = acc
        _row_slabs(xn_ref, w_ref, emit)


def _attn_proj(x, norm_g, w, q_g, k_g, cos_t, sa_t, sb_t, *, tm, tn):
    m = x.shape[0]
    publish = w.dtype != BF16
    assert not publish or m == tm
    nblk = D_MODEL // tn
    tab_blocks = cos_t.shape[0] // tm
    tab_spec = pl.BlockSpec((tm, LANES), lambda i, j: (i % tab_blocks, 0))
    small = lambda shape: pl.BlockSpec(shape, lambda i, j: (0, 0))
    out_sd = lambda dt: jax.ShapeDtypeStruct((m, D_MODEL), dt)
    out_spec = lambda g: pl.BlockSpec((tm, tn), _group_map(g, nblk))
    w_spec = pl.BlockSpec((D_MODEL, tn), lambda i, j: (0, j))
    return pl.pallas_call(
        functools.partial(_attn_proj_kernel, tn=tn, nblk=nblk),
        out_shape=(out_sd(BF16), out_sd(F32), out_sd(BF16), out_sd(F32), out_sd(BF16), out_sd(F32))
        + ((jax.ShapeDtypeStruct(w.shape, BF16),) if publish else ()),
        grid=(m // tm, 4 * nblk),
        in_specs=[pl.BlockSpec((tm, D_MODEL), lambda i, j: (i, 0)),
                  small((1, D_MODEL)),
                  w_spec,
                  small((2, LANES)), small((2, LANES)),
                  tab_spec, tab_spec, tab_spec],
        out_specs=(out_spec(0), out_spec(1), out_spec(1), out_spec(2), out_spec(2), out_spec(3))
        + ((w_spec,) if publish else ()),
        scratch_shapes=[pltpu.VMEM((tm, D_MODEL), BF16)],
        compiler_params=pltpu.CompilerParams(
            dimension_semantics=("arbitrary", "arbitrary"), vmem_limit_bytes=VMEM_LIMIT),
        name="attn_proj",
    )(x, norm_g, w, q_g, k_g, cos_t, sa_t, sb_t)


def _hgrn_proj_kernel(x_ref, g_ref, w_ref, lb_ref, q_ref, k_ref, lf_ref, i_ref, gate_ref, *rest, nblk):
    wb_ref, xn_ref = rest if len(rest) == 2 else (None, rest[0])
    w_ref = _bf16_weights(w_ref, wb_ref)
    j = pl.program_id(1)

    @pl.when(j == 0)
    def _():
        _norm_rows(x_ref, g_ref, xn_ref)

    @pl.when(j < nblk)
    def _():
        def emit(sl, acc):
            q_ref[sl, :] = _silu(acc)
        _row_slabs(xn_ref, w_ref, emit)

    @pl.when((j >= nblk) & (j < 2 * nblk))
    def _():
        def emit(sl, acc):
            lb = lb_ref[...]
            fg = lb + (1.0 - lb) * _sigmoid(acc)
            k_ref[sl, :] = 1.0 - fg
            lf_ref[sl, :] = jnp.log(fg)
        _row_slabs(xn_ref, w_ref, emit)

    @pl.when((j >= 2 * nblk) & (j < 3 * nblk))
    def _():
        def emit(sl, acc):
            i_ref[sl, :] = acc
        _row_slabs(xn_ref, w_ref, emit)

    @pl.when(j >= 3 * nblk)
    def _():
        def emit(sl, acc):
            gate_ref[sl, :] = acc
        _row_slabs(xn_ref, w_ref, emit)


def _hgrn_proj(x, norm_g, w, lb, *, tm, tn):
    m = x.shape[0]
    publish = w.dtype != BF16
    assert not publish or m == tm
    nblk = D_MODEL // tn
    out_sd = jax.ShapeDtypeStruct((m, D_MODEL), F32)
    out_spec = lambda g: pl.BlockSpec((tm, tn), _group_map(g, nblk))
    w_spec = pl.BlockSpec((D_MODEL, tn), lambda i, j: (0, j))
    return pl.pallas_call(
        functools.partial(_hgrn_proj_kernel, nblk=nblk),
        out_shape=(out_sd,) * 5 + ((jax.ShapeDtypeStruct(w.shape, BF16),) if publish else ()),
        grid=(m // tm, 4 * nblk),
        in_specs=[pl.BlockSpec((tm, D_MODEL), lambda i, j: (i, 0)),
                  pl.BlockSpec((1, D_MODEL), lambda i, j: (0, 0)),
                  w_spec,
                  pl.BlockSpec((1, tn), lambda i, j: (0, jnp.clip(j - nblk, 0, nblk - 1)))],
        out_specs=(out_spec(0), out_spec(1), out_spec(1), out_spec(2), out_spec(3))
        + ((w_spec,) if publish else ()),
        scratch_shapes=[pltpu.VMEM((tm, D_MODEL), BF16)],
        compiler_params=pltpu.CompilerParams(
            dimension_semantics=("arbitrary", "arbitrary"), vmem_limit_bytes=VMEM_LIMIT),
        name="hgrn_proj",
    )(x, norm_g, w, lb)


def _out_proj_kernel(a_ref, w_ref, x_ref, y_ref):
    y_ref[...] = x_ref[...] + jnp.dot(a_ref[...], w_ref[...], preferred_element_type=F32)


def _out_proj(a_bf, w_bf, x, *, tm):
    m = x.shape[0]
    row = pl.BlockSpec((tm, D_MODEL), lambda i: (i, 0))
    return pl.pallas_call(
        _out_proj_kernel,
        out_shape=jax.ShapeDtypeStruct((m, D_MODEL), F32),
        grid=(m // tm,),
        in_specs=[row, pl.BlockSpec((D_MODEL, D_MODEL), lambda i: (0, 0), pipeline_mode=pl.Buffered(1)), row],
        out_specs=row,
        compiler_params=pltpu.CompilerParams(
            dimension_semantics=("arbitrary",), vmem_limit_bytes=VMEM_LIMIT),
        name="attn_out_proj",
    )(a_bf, w_bf, x)


def _hgrn_out_kernel(o_ref, gate_ref, g_ref, w_ref, x_ref, y_ref):
    o = o_ref[...]
    ms = jnp.mean(o * o, axis=-1, keepdims=True)
    a = (o * lax.rsqrt(ms + EPS) * g_ref[...]) * _silu(gate_ref[...])
    y_ref[...] = x_ref[...] + jnp.dot(a.astype(BF16), w_ref[...], preferred_element_type=F32)


def _hgrn_out(o, gate, out_g, w_bf, x, *, tm):
    m = x.shape[0]
    row = pl.BlockSpec((tm, D_MODEL), lambda i: (i, 0))
    return pl.pallas_call(
        _hgrn_out_kernel,
        out_shape=jax.ShapeDtypeStruct((m, D_MODEL), F32),
        grid=(m // tm,),
        in_specs=[row, row, pl.BlockSpec((1, D_MODEL), lambda i: (0, 0)),
                  pl.BlockSpec((D_MODEL, D_MODEL), lambda i: (0, 0), pipeline_mode=pl.Buffered(1)), row],
        out_specs=row,
        compiler_params=pltpu.CompilerParams(
            dimension_semantics=("arbitrary",), vmem_limit_bytes=VMEM_LIMIT),
        name="hgrn_out_proj",
    )(o, gate, out_g, w_bf, x)


def _lane_tiles(x):
    return [x[:, j * LANES:(j + 1) * LANES] for j in range(x.shape[1] // LANES)]


def _subln_gate(o, gate, sg_ref):
    ms = jnp.mean(o * o, axis=-1, keepdims=True)
    o = (o * lax.rsqrt(ms + EPS) * sg_ref[...]) * (1.0 - LAMBDA_INIT_0)
    return o * _silu(gate)


def _flash_kernel(lam_ref, q_ref, k_ref, v_ref, gate_ref, sg_ref, o_ref, m0, m1, l0, l1, a0, a1, *, tq):
    m_refs, l_refs, acc_refs = (m0, m1), (l0, l1), (a0, a1)
    qi = pl.program_id(2)
    for c in range(2):
        m_refs[c][...] = jnp.full(m_refs[c].shape, NEG_BIG, F32)
        l_refs[c][...] = jnp.zeros(l_refs[c].shape, F32)
        acc_refs[c][...] = jnp.zeros(acc_refs[c].shape, F32)

    def step(ki, masked):
        kv_rows = pl.ds(pl.multiple_of(ki * tq, tq), tq)
        v_bf = v_ref[kv_rows, :]
        comps = [slice(c * LANES, (c + 1) * LANES) for c in range(2)]
        scores = [lax.dot_general(q_ref[:, comp], k_ref[kv_rows, comp], NT_DIMS, preferred_element_type=F32)
                  for comp in comps]
        probs, alphas = [], []
        for c in range(2):
            s = scores[c]
            if masked:
                r = lax.broadcasted_iota(jnp.int32, (tq, tq), 0)
                col = lax.broadcasted_iota(jnp.int32, (tq, tq), 1)
                s = jnp.where(col <= r, s, NEG_BIG)
            tiles = _lane_tiles(s)
            m_prev = m_refs[c][...]
            m_new = jnp.maximum(m_prev, jnp.max(functools.reduce(jnp.maximum, tiles), axis=-1, keepdims=True))
            alpha = jnp.exp2(m_prev - m_new)
            p_tiles = [jnp.exp2(t - m_new) for t in tiles]
            l_refs[c][...] = alpha * l_refs[c][...] + functools.reduce(jnp.add, p_tiles)
            m_refs[c][...] = m_new
            probs.append(jnp.concatenate(p_tiles, axis=-1).astype(BF16))
            alphas.append(alpha)
        pv = jnp.dot(jnp.concatenate(probs, axis=0), v_bf, preferred_element_type=F32)
        for c in range(2):
            acc_refs[c][...] = (jnp.concatenate([alphas[c], alphas[c]], axis=-1) * acc_refs[c][...]
                                + pv[c * tq:(c + 1) * tq, :])

    def body(ki, carry):
        step(ki, False)
        return carry

    lax.fori_loop(0, qi, body, 0)
    step(qi, True)

    norm = [acc_refs[c][...] / jnp.sum(l_refs[c][...], axis=-1, keepdims=True) for c in range(2)]
    o = norm[0] - lam_ref[0] * norm[1]
    o_ref[...] = _subln_gate(o, gate_ref[...], sg_ref).astype(BF16)


def _flash_attention(lam, q_bf, k_bf, v_bf, gate, subln_g, *, batch, seq, tq):
    nq = seq // tq
    qmap = lambda b, h, qi: (b * nq + qi, h)
    kvmap = lambda b, h, qi: (b, h)
    stat = pltpu.VMEM((tq, LANES), F32)
    acc = pltpu.VMEM((tq, DA_V_DIM), F32)
    return pl.pallas_call(
        functools.partial(_flash_kernel, tq=tq),
        out_shape=jax.ShapeDtypeStruct((batch * seq, DA_HEADS * DA_V_DIM), BF16),
        grid=(batch, DA_HEADS, nq),
        in_specs=[pl.BlockSpec(memory_space=pltpu.SMEM),
                  pl.BlockSpec((tq, DA_V_DIM), qmap),
                  pl.BlockSpec((seq, DA_V_DIM), kvmap),
                  pl.BlockSpec((seq, DA_V_DIM), kvmap),
                  pl.BlockSpec((tq, DA_V_DIM), qmap),
                  pl.BlockSpec((1, DA_V_DIM), lambda b, h, qi: (0, 0))],
        out_specs=pl.BlockSpec((tq, DA_V_DIM), qmap),
        scratch_shapes=[stat, stat, stat, stat, acc, acc],
        compiler_params=pltpu.CompilerParams(
            dimension_semantics=("arbitrary", "arbitrary", "arbitrary"), vmem_limit_bytes=VMEM_LIMIT),
        name="flash_diff_attn",
    )(lam, q_bf, k_bf, v_bf, gate, subln_g)


SAMPLE_ROWS = 64
PAGE_ROWS = PAGE_SIZE * DA_HEADS
PAGES_PER_STEP = 8


def _sample_attn_kernel(pt_ref, lam_ref, qm_ref, *rest, n_steps):
    g = PAGES_PER_STEP
    k_refs, v_refs = rest[:g], rest[g:2 * g]
    bias_ref, kn_ref, vn_ref, biasn_ref, gate_ref, sg_ref, o_ref, m_ref, l_ref, acc_ref = rest[2 * g:]
    step = pl.program_id(1)

    @pl.when(step == 0)
    def _():
        m_ref[...] = jnp.full(m_ref.shape, NEG_BIG, F32)
        l_ref[...] = jnp.zeros(l_ref.shape, F32)
        acc_ref[...] = jnp.zeros(acc_ref.shape, F32)

    def scores(kf, bias):
        return lax.dot_general(qm_ref[0], kf.astype(BF16), NT_DIMS, preferred_element_type=F32) + bias

    flat = lambda ref: ref[0, 0].reshape(PAGE_ROWS, DA_V_DIM)
    parts = []
    for i in range(g):
        tiles = _lane_tiles(scores(flat(k_refs[i]), bias_ref[...]))
        m_i = jnp.broadcast_to(jnp.max(functools.reduce(jnp.maximum, tiles), axis=-1, keepdims=True),
                               (SAMPLE_ROWS, LANES))
        p_tiles = [jnp.exp2(t - m_i) for t in tiles]
        pv_i = jnp.dot(jnp.concatenate(p_tiles, axis=-1).astype(BF16), flat(v_refs[i]).astype(BF16),
                       preferred_element_type=F32)
        parts.append((m_i, functools.reduce(jnp.add, p_tiles), pv_i))
    m_prev = m_ref[...]
    m_new = functools.reduce(jnp.maximum, [m_prev] + [part[0] for part in parts])
    alpha = jnp.exp2(m_prev - m_new)
    l_new = alpha * l_ref[...]
    acc_new = jnp.concatenate([alpha, alpha], axis=-1) * acc_ref[...]
    for m_i, l_i, pv_i in parts:
        w = jnp.exp2(m_i - m_new)
        l_new = l_new + w * l_i
        acc_new = acc_new + jnp.concatenate([w, w], axis=-1) * pv_i
    l_ref[...] = l_new
    acc_ref[...] = acc_new
    m_ref[...] = m_new

    @pl.when(step == n_steps - 1)
    def _():
        m_old = m_ref[:, 0:1]
        l_old = jnp.sum(l_ref[...], axis=-1, keepdims=True)
        s = scores(kn_ref[0], biasn_ref[...])
        m_fin = jnp.maximum(m_old, jnp.max(s, axis=-1, keepdims=True))
        a_fin = jnp.exp2(m_old - m_fin)
        p = jnp.exp2(s - m_fin)
        l_fin = a_fin * l_old + jnp.sum(p, axis=-1, keepdims=True)
        acc = a_fin * acc_ref[...] + jnp.dot(p.astype(BF16), vn_ref[0].astype(BF16), preferred_element_type=F32)
        norm = acc / l_fin
        half = SAMPLE_ROWS // 2
        o = norm[0:half, :] - lam_ref[0] * norm[half:, :]
        o_ref[0] = _subln_gate(o, gate_ref[0], sg_ref).astype(BF16)


def _sample_attention(page_table, lam, qm, cache_k, cache_v, k_new, v_new, gate_r, subln_g, *, dec_seq):
    dec_batch, n_pages = page_table.shape
    g = PAGES_PER_STEP
    n_steps = n_pages // g
    rows_new = dec_seq * DA_HEADS
    row_head = (jnp.arange(SAMPLE_ROWS) // dec_seq) % DA_HEADS
    col = jnp.arange(PAGE_ROWS)
    bias = jnp.where((col % DA_HEADS)[None, :] == row_head[:, None], 0.0, NEG_BIG).astype(F32)
    col_n = jnp.arange(rows_new)
    valid_n = ((col_n % DA_HEADS)[None, :] == row_head[:, None]) & (
        (col_n // DA_HEADS)[None, :] <= (jnp.arange(SAMPLE_ROWS) % dec_seq)[:, None])
    bias_n = jnp.where(valid_n, 0.0, NEG_BIG).astype(F32)

    def page_spec(i):
        return pl.BlockSpec((1, 1, PAGE_SIZE, DA_HEADS, DA_V_DIM),
                            lambda b, p, pt: (0, pt[b, p * g + i], 0, 0, 0))

    per_b = lambda rows, cols: pl.BlockSpec((1, rows, cols), lambda b, p, pt: (b, 0, 0))
    const = lambda shape: pl.BlockSpec(shape, lambda b, p, pt: (0, 0))
    grid_spec = pltpu.PrefetchScalarGridSpec(
        num_scalar_prefetch=1,
        grid=(dec_batch, n_steps),
        in_specs=[pl.BlockSpec(memory_space=pltpu.SMEM), per_b(SAMPLE_ROWS, DA_V_DIM)]
        + [page_spec(i) for i in range(g)] + [page_spec(i) for i in range(g)]
        + [const((SAMPLE_ROWS, PAGE_ROWS)), per_b(rows_new, DA_V_DIM), per_b(rows_new, DA_V_DIM),
           const((SAMPLE_ROWS, rows_new)), per_b(rows_new, DA_V_DIM), const((1, DA_V_DIM))],
        out_specs=per_b(rows_new, DA_V_DIM),
        scratch_shapes=[pltpu.VMEM((SAMPLE_ROWS, LANES), F32), pltpu.VMEM((SAMPLE_ROWS, LANES), F32),
                        pltpu.VMEM((SAMPLE_ROWS, DA_V_DIM), F32)],
    )
    return pl.pallas_call(
        functools.partial(_sample_attn_kernel, n_steps=n_steps),
        out_shape=jax.ShapeDtypeStruct((dec_batch, rows_new, DA_V_DIM), BF16),
        grid_spec=grid_spec,
        compiler_params=pltpu.CompilerParams(
            dimension_semantics=("arbitrary", "arbitrary"), vmem_limit_bytes=VMEM_LIMIT),
        name="paged_diff_attn",
    )(page_table, lam, qm, *([cache_k] * g), *([cache_v] * g), bias, k_new, v_new, bias_n, gate_r, subln_g)


def _tile_cumsum(x, rowi):
    for s in (1, 2, 4):
        x = x + jnp.where(rowi >= s, pltpu.roll(x, s, 0), 0.0)
    return x


def _pair_levels(length, nheads):
    n = length * nheads
    ri = lax.broadcasted_iota(jnp.int32, (n, n), 0)
    ci = lax.broadcasted_iota(jnp.int32, (n, n), 1)
    lvl = jnp.zeros((n, n), jnp.int32)
    shift = 0
    while (1 << shift) < length:
        lvl = lvl + jnp.where((ri >> shift) != (ci >> shift), 1, 0)
        shift += 1
    return jnp.where(((ri >> shift) != (ci >> shift)) | (ci > ri), -1, lvl)


def _hgrn_chunk(q, k, v, lf, states, levels):
    nheads = len(states)
    length = q[0].shape[0]
    nt = length // SUBLANES
    rowi = lax.broadcasted_iota(jnp.int32, (SUBLANES, LANES), 0)
    tile = lambda a, r: a[r * SUBLANES:(r + 1) * SUBLANES, :]
    stack = lambda tiles: jnp.concatenate(tiles, axis=0) if len(tiles) > 1 else tiles[0]
    bcast_row = lambda a, r, n: jnp.broadcast_to(a[r:r + 1, :], (n, LANES))

    b_tiles, b, o_inter, st_new = [], [], [], []
    for h in range(nheads):
        tiles, carry = [], None
        for r in range(nt):
            c = _tile_cumsum(tile(lf[h], r), rowi)
            if carry is not None:
                c = c + carry
            carry = c[SUBLANES - 1:SUBLANES, :]
            tiles.append(c)
        b_tiles.append(tiles)
        b.append(stack(tiles))
        b_last = carry
        qe = (q[h] * jnp.exp(b[h])).astype(BF16)
        o_inter.append(lax.dot_general(qe, states[h].astype(BF16), NT_DIMS, preferred_element_type=F32))
        kd = (k[h] * jnp.exp(b_last - b[h])).astype(BF16)
        st_new.append(states[h] * jnp.exp(b_last)
                      + lax.dot_general(v[h].astype(BF16), kd, TN_DIMS, preferred_element_type=F32))

    a_mat = jnp.where(levels == 0,
                      lax.dot_general(stack(q).astype(BF16), stack(k).astype(BF16), NT_DIMS,
                                      preferred_element_type=F32), 0.0)
    level, half = 1, 1
    while half < length:
        blk = 2 * half
        q_s, k_s = [], []
        for h in range(nheads):
            if half == 1:
                ref_b = stack([jnp.where(rowi % 2 == 1, pltpu.roll(bt, 1, 0), bt) for bt in b_tiles[h]])
            elif blk < SUBLANES:
                ref_b = stack([jnp.where(rowi < blk, bcast_row(bt, half - 1, SUBLANES),
                                         bcast_row(bt, blk + half - 1, SUBLANES)) for bt in b_tiles[h]])
            else:
                ref_b = stack([bcast_row(b[h], b0 + half - 1, blk) for b0 in range(0, length, blk)])
            e = jnp.exp2(jnp.abs(b[h] - ref_b) * -LOG2E)
            q_s.append(q[h] * e)
            k_s.append(k[h] * e)
        s = lax.dot_general(stack(q_s).astype(BF16), stack(k_s).astype(BF16), NT_DIMS,
                            preferred_element_type=F32)
        a_mat = jnp.where(levels == level, s, a_mat)
        level, half = level + 1, blk
    o_intra = jnp.dot(a_mat.astype(BF16), stack(v).astype(BF16), preferred_element_type=F32)
    outs = [o_inter[h] + o_intra[h * length:(h + 1) * length, :] for h in range(nheads)]
    return outs, st_new


HEADS_PER_STEP = 4


def _hgrn_prompt_kernel(q_ref, k_ref, lf_ref, v_ref, o_ref, s_ref, st_ref, *, chunk, nchunks, nblocks):
    tb = pl.program_id(2)

    @pl.when(tb == 0)
    def _():
        st_ref[...] = jnp.zeros(st_ref.shape, F32)

    levels = _pair_levels(chunk, HEADS_PER_STEP)
    head_cols = [slice(hh * HG_DK, (hh + 1) * HG_DK) for hh in range(HEADS_PER_STEP)]

    def body(ci, carry):
        rows = pl.ds(pl.multiple_of(ci * chunk, chunk), chunk)
        per_head = lambda ref: [ref[rows, cols] for cols in head_cols]
        outs, st_new = _hgrn_chunk(per_head(q_ref), per_head(k_ref), per_head(v_ref), per_head(lf_ref),
                                   [st_ref[hh] for hh in range(HEADS_PER_STEP)], levels)
        for hh in range(HEADS_PER_STEP):
            o_ref[rows, head_cols[hh]] = outs[hh]
            st_ref[hh] = st_new[hh]
        return carry

    lax.fori_loop(0, nchunks, body, 0)

    @pl.when(tb == nblocks - 1)
    def _():
        for hh in range(HEADS_PER_STEP):
            s_ref[hh] = st_ref[hh].T


def _hgrn_prompt(q, k, lf, v, *, batch, seq, tblock, chunk):
    nblocks = seq // tblock
    width = HEADS_PER_STEP * HG_DK
    tok = pl.BlockSpec((tblock, width), lambda b, h, t: (b * nblocks + t, h))
    return pl.pallas_call(
        functools.partial(_hgrn_prompt_kernel, chunk=chunk, nchunks=tblock // chunk, nblocks=nblocks),
        out_shape=(jax.ShapeDtypeStruct((batch * seq, D_MODEL), F32),
                   jax.ShapeDtypeStruct((batch, HG_HEADS, HG_DK, HG_DK), F32)),
        grid=(batch, HG_HEADS // HEADS_PER_STEP, nblocks),
        in_specs=[tok, tok, tok, tok],
        out_specs=(tok, pl.BlockSpec((None, HEADS_PER_STEP, HG_DK, HG_DK), lambda b, h, t: (b, h, 0, 0))),
        scratch_shapes=[pltpu.VMEM((HEADS_PER_STEP, HG_DK, HG_DK), F32)],
        compiler_params=pltpu.CompilerParams(
            dimension_semantics=("arbitrary", "arbitrary", "arbitrary"), vmem_limit_bytes=VMEM_LIMIT),
        name="hgrn_prompt_scan",
    )(q, k, lf, v)


def _hgrn_sample_kernel(q_ref, k_ref, lf_ref, v_ref, s0_ref, o_ref, s1_ref):
    levels = _pair_levels(SUBLANES, HEADS_PER_STEP)

    def body(hb, carry):
        heads = [hb * HEADS_PER_STEP + hh for hh in range(HEADS_PER_STEP)]
        per_head = lambda ref: [ref[0, h] for h in heads]
        outs, st_new = _hgrn_chunk(per_head(q_ref), per_head(k_ref), per_head(v_ref), per_head(lf_ref),
                                   [s0_ref[h].T for h in heads], levels)
        for hh, h in enumerate(heads):
            o_ref[0, h] = outs[hh]
            s1_ref[h] = st_new[hh].T
        return carry

    lax.fori_loop(0, HG_HEADS // HEADS_PER_STEP, body, 0)


def _hgrn_sample(q, k, lf, v, state):
    dec_batch = q.shape[0]
    tok = pl.BlockSpec((1, HG_HEADS, SUBLANES, HG_DK), lambda b: (b, 0, 0, 0))
    st = pl.BlockSpec((None, HG_HEADS, HG_DK, HG_DK), lambda b: (b, 0, 0, 0))
    return pl.pallas_call(
        _hgrn_sample_kernel,
        out_shape=(jax.ShapeDtypeStruct(q.shape, F32), jax.ShapeDtypeStruct(state.shape, F32)),
        grid=(dec_batch,),
        in_specs=[tok, tok, tok, tok, st],
        out_specs=(tok, st),
        compiler_params=pltpu.CompilerParams(
            dimension_semantics=("arbitrary",), vmem_limit_bytes=VMEM_LIMIT),
        name="hgrn_sample_step",
    )(q, k, lf, v, state)


def _rope_tables(pos):
    inv = jnp.power(ROPE_THETA, -jnp.arange(0, ROPE_DIM, 2, dtype=F32) / ROPE_DIM)
    ang = pos.astype(F32)[:, None] * inv[None, :]
    cos, sin = jnp.cos(ang), jnp.sin(ang)
    n = pos.shape[0]
    pad1 = jnp.ones((n, LANES - ROPE_DIM), F32)
    pad0 = jnp.zeros((n, LANES - ROPE_DIM), F32)
    zero_h = jnp.zeros_like(sin)
    cos_t = jnp.concatenate([cos, cos, pad1], axis=1)
    sa_t = jnp.concatenate([zero_h, sin, pad0], axis=1)
    sb_t = jnp.concatenate([-sin, zero_h, pad0], axis=1)
    return cos_t, sa_t, sb_t


def kernel(x_prompt, x_sample, cache_k, cache_v, state_hgrn, page_table, attn_norm, attn_w_in, attn_q_norm,
           attn_k_norm, attn_lambda, attn_subln, attn_w_out, hgrn_norm, hgrn_w_in, hgrn_lower_bounds,
           hgrn_out_norm, hgrn_w_out):
    batch, seq, _ = x_prompt.shape
    dec_batch, dec_seq, _ = x_sample.shape
    mp, ms = batch * seq, dec_batch * dec_seq
    xp = x_prompt.reshape(mp, D_MODEL)
    xs = x_sample.reshape(ms, D_MODEL)

    lq1, lk1, lq2, lk2 = attn_lambda[0].astype(F32)
    lam = (jnp.exp(jnp.sum(lq1 * lk1)) - jnp.exp(jnp.sum(lq2 * lk2)) + LAMBDA_INIT_0).reshape(1)
    w_out = attn_w_out[0].astype(BF16)
    norm_g = attn_norm[0].reshape(1, D_MODEL)
    subln_g = attn_subln[0].reshape(1, DA_V_DIM)

    tabs_p = _rope_tables(jnp.arange(seq))
    tabs_s = _rope_tables(jnp.tile(PAST_LEN + jnp.arange(dec_seq), dec_batch))

    q_s, k_s, _, v_s, _, g_s, w_in = _attn_proj(xs, norm_g, attn_w_in[0], attn_q_norm[0], attn_k_norm[0], *tabs_s,
                                                tm=ms, tn=1024)
    q_p, k_p, kb_p, v_p, vb_p, g_p = _attn_proj(xp, norm_g, w_in, attn_q_norm[0], attn_k_norm[0], *tabs_p,
                                                tm=1024, tn=512)
    og_p = _flash_attention(lam, q_p, kb_p, vb_p, g_p, subln_g, batch=batch, seq=seq, tq=512)
    xp = _out_proj(og_p, w_out, xp, tm=512)

    q5 = q_s.reshape(dec_batch, dec_seq, DA_HEADS, 2, DA_HEAD_DIM).transpose(0, 3, 2, 1, 4)
    zeros = jnp.zeros_like(q5[:, 0])
    qm = jnp.stack([jnp.concatenate([q5[:, 0], zeros], axis=-1),
                    jnp.concatenate([zeros, q5[:, 1]], axis=-1)], axis=1)
    qm = qm.reshape(dec_batch, SAMPLE_ROWS, DA_V_DIM)
    rows_new = dec_seq * DA_HEADS
    gate_r = g_s.reshape(dec_batch, dec_seq, DA_HEADS, DA_V_DIM).transpose(0, 2, 1, 3)
    gate_r = gate_r.reshape(dec_batch, rows_new, DA_V_DIM)
    og_s = _sample_attention(page_table, lam, qm, cache_k, cache_v,
                             k_s.reshape(dec_batch, rows_new, DA_V_DIM),
                             v_s.reshape(dec_batch, rows_new, DA_V_DIM), gate_r, subln_g, dec_seq=dec_seq)
    og_s = og_s.reshape(dec_batch, DA_HEADS, dec_seq, DA_V_DIM).transpose(0, 2, 1, 3).reshape(ms, D_MODEL)
    xs = _out_proj(og_s, w_out, xs, tm=ms)

    k_prompt = k_p.reshape(1, batch, seq, DA_HEADS, 2 * DA_HEAD_DIM)
    v_prompt = v_p.reshape(1, batch, seq, DA_HEADS, DA_V_DIM)
    k_sample = k_s.reshape(1, dec_batch, dec_seq, DA_HEADS, 2 * DA_HEAD_DIM)
    v_sample = v_s.reshape(1, dec_batch, dec_seq, DA_HEADS, DA_V_DIM)

    pr = jax.nn.softmax(hgrn_lower_bounds.astype(F32), axis=0)
    lb = (jnp.cumsum(pr, axis=0)[1] - pr[0]).reshape(1, D_MODEL)
    hw_out = hgrn_w_out[0].astype(BF16)
    hnorm_g = hgrn_norm[0].reshape(1, D_MODEL)
    hout_g = hgrn_out_norm[0].reshape(1, D_MODEL)

    sq, sk, slf, sv, sg, hw_in = _hgrn_proj(xs, hnorm_g, hgrn_w_in[0], lb, tm=ms, tn=1024)
    hq, hk, hlf, hv, hg = _hgrn_proj(xp, hnorm_g, hw_in, lb, tm=1024, tn=512)
    ho, s_prompt = _hgrn_prompt(hq, hk, hlf, hv, batch=batch, seq=seq, tblock=512, chunk=64)
    xp = _hgrn_out(ho, hg, hout_g, hw_out, xp, tm=512)

    def to_heads(a):
        a = a.reshape(dec_batch, dec_seq, HG_HEADS, HG_DK).transpose(0, 2, 1, 3)
        return jnp.pad(a, ((0, 0), (0, 0), (0, SUBLANES - dec_seq), (0, 0)))

    so, s_sample = _hgrn_sample(to_heads(sq), to_heads(sk), to_heads(slf), to_heads(sv), state_hgrn[0])
    so = so[:, :, :dec_seq].transpose(0, 2, 1, 3).reshape(ms, D_MODEL)
    xs = _hgrn_out(so, sg, hout_g, hw_out, xs, tm=ms)

    return (xp.reshape(batch, seq, D_MODEL), xs.reshape(dec_batch, dec_seq, D_MODEL),
            k_prompt, v_prompt, k_sample, v_sample, s_prompt[None], s_sample[None])
```

```python
import functools
import math

import jax
import jax.numpy as jnp
from jax import lax
from jax.experimental import pallas as pl
from jax.experimental.pallas import tpu as pltpu

F32 = jnp.float32
BF16 = jnp.bfloat16

D_MODEL = 2048
PAST_LEN = 8192
PAGE_SIZE = 128
DA_HEADS = 8
DA_HEAD_DIM = 128
DA_V_DIM = 256
ROPE_DIM = 32
ROPE_THETA = 500000.0
HG_HEADS = 16
HG_DK = 128
EPS = 1e-6
LAMBDA_INIT_0 = 0.8 - 0.6 * math.exp(-0.3 * 0)

LANES = 128
SUBLANES = 8
VMEM_LIMIT = 56 * 1024 * 1024
NEG_BIG = -1e30
LOG2E = math.log2(math.e)
EXP2_SCALE = DA_HEAD_DIM ** -0.5 * LOG2E

NT_DIMS = (((1,), (1,)), ((), ()))
TN_DIMS = (((0,), (0,)), ((), ()))


def _silu(x):
    return x * (1.0 / (1.0 + jnp.exp(-x)))


def _sigmoid(x):
    return 1.0 / (1.0 + jnp.exp(-x))


def _group_map(group, nblk):
    return lambda i, j: (i, jnp.clip(j - group * nblk, 0, nblk - 1))


def _norm_rows(x_ref, g_ref, xn_ref):
    x = x_ref[...]
    ms = jnp.mean(x * x, axis=-1, keepdims=True)
    xn_ref[...] = (x * lax.rsqrt(ms + EPS) * g_ref[...]).astype(BF16)


ROW_SPLITS = 4
MIN_SLAB_ROWS = 256


def _row_slabs(xn_ref, w_ref, emit):
    rows = min(xn_ref.shape[0], max(MIN_SLAB_ROWS, xn_ref.shape[0] // ROW_SPLITS))
    for r in range(xn_ref.shape[0] // rows):
        sl = slice(r * rows, (r + 1) * rows)
        emit(sl, jnp.dot(xn_ref[sl, :], w_ref[...], preferred_element_type=F32))


def _bf16_weights(w_ref, wb_ref):
    if wb_ref is None:
        return w_ref
    wb_ref[...] = w_ref[...].astype(BF16)
    return wb_ref


def _attn_proj_kernel(x_ref, g_ref, w_ref, qg_ref, kg_ref, cos_ref, sa_ref, sb_ref,
                      q_ref, k_ref, kb_ref, v_ref, vb_ref, gate_ref, *rest, tn, nblk):
    wb_ref, xn_ref = rest if len(rest) == 2 else (None, rest[0])
    w_ref = _bf16_weights(w_ref, wb_ref)
    j = pl.program_id(1)

    @pl.when(j == 0)
    def _():
        _norm_rows(x_ref, g_ref, xn_ref)

    def qk_post(acc, sl, gain_ref, emit):
        for idx in range(tn // LANES):
            c = idx % 2
            a = acc[:, idx * LANES:(idx + 1) * LANES]
            ms = jnp.mean(a * a, axis=-1, keepdims=True)
            y = a * lax.rsqrt(ms + EPS) * gain_ref[c:c + 1, :]
            y = (y * cos_ref[sl, :] + pltpu.roll(y, ROPE_DIM // 2, 1) * sa_ref[sl, :]
                 + pltpu.roll(y, LANES - ROPE_DIM // 2, 1) * sb_ref[sl, :])
            emit(slice(idx * LANES, (idx + 1) * LANES), y)

    @pl.when(j < nblk)
    def _():
        def emit(sl, acc):
            def put(cols, y):
                q_ref[sl, cols] = (y * EXP2_SCALE).astype(BF16)
            qk_post(acc, sl, qg_ref, put)
        _row_slabs(xn_ref, w_ref, emit)

    @pl.when((j >= nblk) & (j < 2 * nblk))
    def _():
        def emit(sl, acc):
            def put(cols, y):
                k_ref[sl, cols] = y
                kb_ref[sl, cols] = y.astype(BF16)
            qk_post(acc, sl, kg_ref, put)
        _row_slabs(xn_ref, w_ref, emit)

    @pl.when((j >= 2 * nblk) & (j < 3 * nblk))
    def _():
        def emit(sl, acc):
            v_ref[sl, :] = acc
            vb_ref[sl, :] = acc.astype(BF16)
        _row_slabs(xn_ref, w_ref, emit)

    @pl.when(j >= 3 * nblk)
    def _():
        def emit(sl, acc):
            gate_ref[sl, :] = acc
        _row_slabs(xn_ref, w_ref, emit)


def _attn_proj(x, norm_g, w, q_g, k_g, cos_t, sa_t, sb_t, *, tm, tn):
    m = x.shape[0]
    publish = w.dtype != BF16
    assert not publish or m == tm
    nblk = D_MODEL // tn
    tab_blocks = cos_t.shape[0] // tm
    tab_spec = pl.BlockSpec((tm, LANES), lambda i, j: (i % tab_blocks, 0))
    small = lambda shape: pl.BlockSpec(shape, lambda i, j: (0, 0))
    out_sd = lambda dt: jax.ShapeDtypeStruct((m, D_MODEL), dt)
    out_spec = lambda g: pl.BlockSpec((tm, tn), _group_map(g, nblk))
    w_spec = pl.BlockSpec((D_MODEL, tn), lambda i, j: (0, j))
    return pl.pallas_call(
        functools.partial(_attn_proj_kernel, tn=tn, nblk=nblk),
        out_shape=(out_sd(BF16), out_sd(F32), out_sd(BF16), out_sd(F32), out_sd(BF16), out_sd(F32))
        + ((jax.ShapeDtypeStruct(w.shape, BF16),) if publish else ()),
        grid=(m // tm, 4 * nblk),
        in_specs=[pl.BlockSpec((tm, D_MODEL), lambda i, j: (i, 0)),
                  small((1, D_MODEL)),
                  w_spec,
                  small((2, LANES)), small((2, LANES)),
                  tab_spec, tab_spec, tab_spec],
        out_specs=(out_spec(0), out_spec(1), out_spec(1), out_spec(2), out_spec(2), out_spec(3))
        + ((w_spec,) if publish else ()),
        scratch_shapes=[pltpu.VMEM((tm, D_MODEL), BF16)],
        compiler_params=pltpu.CompilerParams(
            dimension_semantics=("arbitrary", "arbitrary"), vmem_limit_bytes=VMEM_LIMIT),
        name="attn_proj",
    )(x, norm_g, w, q_g, k_g, cos_t, sa_t, sb_t)


def _hgrn_proj_kernel(x_ref, g_ref, w_ref, lb_ref, q_ref, k_ref, lf_ref, i_ref, gate_ref, *rest, nblk):
    wb_ref, xn_ref = rest if len(rest) == 2 else (None, rest[0])
    w_ref = _bf16_weights(w_ref, wb_ref)
    j = pl.program_id(1)

    @pl.when(j == 0)
    def _():
        _norm_rows(x_ref, g_ref, xn_ref)

    @pl.when(j < nblk)
    def _():
        def emit(sl, acc):
            q_ref[sl, :] = _silu(acc)
        _row_slabs(xn_ref, w_ref, emit)

    @pl.when((j >= nblk) & (j < 2 * nblk))
    def _():
        def emit(sl, acc):
            lb = lb_ref[...]
            fg = lb + (1.0 - lb) * _sigmoid(acc)
            k_ref[sl, :] = 1.0 - fg
            lf_ref[sl, :] = jnp.log(fg)
        _row_slabs(xn_ref, w_ref, emit)

    @pl.when((j >= 2 * nblk) & (j < 3 * nblk))
    def _():
        def emit(sl, acc):
            i_ref[sl, :] = acc
        _row_slabs(xn_ref, w_ref, emit)

    @pl.when(j >= 3 * nblk)
    def _():
        def emit(sl, acc):
            gate_ref[sl, :] = acc
        _row_slabs(xn_ref, w_ref, emit)


def _hgrn_proj(x, norm_g, w, lb, *, tm, tn):
    m = x.shape[0]
    publish = w.dtype != BF16
    assert not publish or m == tm
    nblk = D_MODEL // tn
    out_sd = jax.ShapeDtypeStruct((m, D_MODEL), F32)
    out_spec = lambda g: pl.BlockSpec((tm, tn), _group_map(g, nblk))
    w_spec = pl.BlockSpec((D_MODEL, tn), lambda i, j: (0, j))
    return pl.pallas_call(
        functools.partial(_hgrn_proj_kernel, nblk=nblk),
        out_shape=(out_sd,) * 5 + ((jax.ShapeDtypeStruct(w.shape, BF16),) if publish else ()),
        grid=(m // tm, 4 * nblk),
        in_specs=[pl.BlockSpec((tm, D_MODEL), lambda i, j: (i, 0)),
                  pl.BlockSpec((1, D_MODEL), lambda i, j: (0, 0)),
                  w_spec,
                  pl.BlockSpec((1, tn), lambda i, j: (0, jnp.clip(j - nblk, 0, nblk - 1)))],
        out_specs=(out_spec(0), out_spec(1), out_spec(1), out_spec(2), out_spec(3))
        + ((w_spec,) if publish else ()),
        scratch_shapes=[pltpu.VMEM((tm, D_MODEL), BF16)],
        compiler_params=pltpu.CompilerParams(
            dimension_semantics=("arbitrary", "arbitrary"), vmem_limit_bytes=VMEM_LIMIT),
        name="hgrn_proj",
    )(x, norm_g, w, lb)


def _out_proj_kernel(a_ref, w_ref, x_ref, y_ref):
    y_ref[...] = x_ref[...] + jnp.dot(a_ref[...], w_ref[...], preferred_element_type=F32)


def _out_proj(a_bf, w_bf, x, *, tm):
    m = x.shape[0]
    row = pl.BlockSpec((tm, D_MODEL), lambda i: (i, 0))
    return pl.pallas_call(
        _out_proj_kernel,
        out_shape=jax.ShapeDtypeStruct((m, D_MODEL), F32),
        grid=(m // tm,),
        in_specs=[row, pl.BlockSpec((D_MODEL, D_MODEL), lambda i: (0, 0), pipeline_mode=pl.Buffered(1)), row],
        out_specs=row,
        compiler_params=pltpu.CompilerParams(
            dimension_semantics=("arbitrary",), vmem_limit_bytes=VMEM_LIMIT),
        name="attn_out_proj",
    )(a_bf, w_bf, x)


def _hgrn_out_kernel(o_ref, gate_ref, g_ref, w_ref, x_ref, y_ref):
    o = o_ref[...]
    ms = jnp.mean(o * o, axis=-1, keepdims=True)
    a = (o * lax.rsqrt(ms + EPS) * g_ref[...]) * _silu(gate_ref[...])
    y_ref[...] = x_ref[...] + jnp.dot(a.astype(BF16), w_ref[...], preferred_element_type=F32)


def _hgrn_out(o, gate, out_g, w_bf, x, *, tm):
    m = x.shape[0]
    row = pl.BlockSpec((tm, D_MODEL), lambda i: (i, 0))
    return pl.pallas_call(
        _hgrn_out_kernel,
        out_shape=jax.ShapeDtypeStruct((m, D_MODEL), F32),
        grid=(m // tm,),
        in_specs=[row, row, pl.BlockSpec((1, D_MODEL), lambda i: (0, 0)),
                  pl.BlockSpec((D_MODEL, D_MODEL), lambda i: (0, 0), pipeline_mode=pl.Buffered(1)), row],
        out_specs=row,
        compiler_params=pltpu.CompilerParams(
            dimension_semantics=("arbitrary",), vmem_limit_bytes=VMEM_LIMIT),
        name="hgrn_out_proj",
    )(o, gate, out_g, w_bf, x)


def _lane_tiles(x):
    return [x[:, j * LANES:(j + 1) * LANES] for j in range(x.shape[1] // LANES)]


def _subln_gate(o, gate, sg_ref):
    ms = jnp.mean(o * o, axis=-1, keepdims=True)
    o = (o * lax.rsqrt(ms + EPS) * sg_ref[...]) * (1.0 - LAMBDA_INIT_0)
    return o * _silu(gate)


def _flash_kernel(lam_ref, q_ref, k_ref, v_ref, gate_ref, sg_ref, o_ref, m0, m1, l0, l1, a0, a1, *, tq):
    m_refs, l_refs, acc_refs = (m0, m1), (l0, l1), (a0, a1)
    qi = pl.program_id(2)
    for c in range(2):
        m_refs[c][...] = jnp.full(m_refs[c].shape, NEG_BIG, F32)
        l_refs[c][...] = jnp.zeros(l_refs[c].shape, F32)
        acc_refs[c][...] = jnp.zeros(acc_refs[c].shape, F32)

    def step(ki, masked):
        kv_rows = pl.ds(pl.multiple_of(ki * tq, tq), tq)
        v_bf = v_ref[kv_rows, :]
        comps = [slice(c * LANES, (c + 1) * LANES) for c in range(2)]
        scores = [lax.dot_general(q_ref[:, comp], k_ref[kv_rows, comp], NT_DIMS, preferred_element_type=F32)
                  for comp in comps]
        probs, alphas = [], []
        for c in range(2):
            s = scores[c]
            if masked:
                r = lax.broadcasted_iota(jnp.int32, (tq, tq), 0)
                col = lax.broadcasted_iota(jnp.int32, (tq, tq), 1)
                s = jnp.where(col <= r, s, NEG_BIG)
            tiles = _lane_tiles(s)
            m_prev = m_refs[c][...]
            m_new = jnp.maximum(m_prev, jnp.max(functools.reduce(jnp.maximum, tiles), axis=-1, keepdims=True))
            alpha = jnp.exp2(m_prev - m_new)
            p_tiles = [jnp.exp2(t - m_new) for t in tiles]
            l_refs[c][...] = alpha * l_refs[c][...] + functools.reduce(jnp.add, p_tiles)
            m_refs[c][...] = m_new
            probs.append(jnp.concatenate(p_tiles, axis=-1).astype(BF16))
            alphas.append(alpha)
        pv = jnp.dot(jnp.concatenate(probs, axis=0), v_bf, preferred_element_type=F32)
        for c in range(2):
            acc_refs[c][...] = (jnp.concatenate([alphas[c], alphas[c]], axis=-1) * acc_refs[c][...]
                                + pv[c * tq:(c + 1) * tq, :])

    def body(ki, carry):
        step(ki, False)
        return carry

    lax.fori_loop(0, qi, body, 0)
    step(qi, True)

    norm = [acc_refs[c][...] / jnp.sum(l_refs[c][...], axis=-1, keepdims=True) for c in range(2)]
    o = norm[0] - lam_ref[0] * norm[1]
    o_ref[...] = _subln_gate(o, gate_ref[...], sg_ref).astype(BF16)


def _flash_attention(lam, q_bf, k_bf, v_bf, gate, subln_g, *, batch, seq, tq):
    nq = seq // tq
    qmap = lambda b, h, qi: (b * nq + qi, h)
    kvmap = lambda b, h, qi: (b, h)
    stat = pltpu.VMEM((tq, LANES), F32)
    acc = pltpu.VMEM((tq, DA_V_DIM), F32)
    return pl.pallas_call(
        functools.partial(_flash_kernel, tq=tq),
        out_shape=jax.ShapeDtypeStruct((batch * seq, DA_HEADS * DA_V_DIM), BF16),
        grid=(batch, DA_HEADS, nq),
        in_specs=[pl.BlockSpec(memory_space=pltpu.SMEM),
                  pl.BlockSpec((tq, DA_V_DIM), qmap),
                  pl.BlockSpec((seq, DA_V_DIM), kvmap),
                  pl.BlockSpec((seq, DA_V_DIM), kvmap),
                  pl.BlockSpec((tq, DA_V_DIM), qmap),
                  pl.BlockSpec((1, DA_V_DIM), lambda b, h, qi: (0, 0))],
        out_specs=pl.BlockSpec((tq, DA_V_DIM), qmap),
        scratch_shapes=[stat, stat, stat, stat, acc, acc],
        compiler_params=pltpu.CompilerParams(
            dimension_semantics=("arbitrary", "arbitrary", "arbitrary"), vmem_limit_bytes=VMEM_LIMIT),
        name="flash_diff_attn",
    )(lam, q_bf, k_bf, v_bf, gate, subln_g)


SAMPLE_ROWS = 64
PAGE_ROWS = PAGE_SIZE * DA_HEADS
PAGES_PER_STEP = 8


def _sample_attn_kernel(pt_ref, lam_ref, qm_ref, *rest, n_steps):
    g = PAGES_PER_STEP
    k_refs, v_refs = rest[:g], rest[g:2 * g]
    bias_ref, kn_ref, vn_ref, biasn_ref, gate_ref, sg_ref, o_ref, m_ref, l_ref, acc_ref = rest[2 * g:]
    step = pl.program_id(1)

    @pl.when(step == 0)
    def _():
        m_ref[...] = jnp.full(m_ref.shape, NEG_BIG, F32)
        l_ref[...] = jnp.zeros(l_ref.shape, F32)
        acc_ref[...] = jnp.zeros(acc_ref.shape, F32)

    def scores(kf, bias):
        return lax.dot_general(qm_ref[0], kf.astype(BF16), NT_DIMS, preferred_element_type=F32) + bias

    flat = lambda ref: ref[0, 0].reshape(PAGE_ROWS, DA_V_DIM)
    parts = []
    for i in range(g):
        tiles = _lane_tiles(scores(flat(k_refs[i]), bias_ref[...]))
        m_i = jnp.broadcast_to(jnp.max(functools.reduce(jnp.maximum, tiles), axis=-1, keepdims=True),
                               (SAMPLE_ROWS, LANES))
        p_tiles = [jnp.exp2(t - m_i) for t in tiles]
        pv_i = jnp.dot(jnp.concatenate(p_tiles, axis=-1).astype(BF16), flat(v_refs[i]).astype(BF16),
                       preferred_element_type=F32)
        parts.append((m_i, functools.reduce(jnp.add, p_tiles), pv_i))
    m_prev = m_ref[...]
    m_new = functools.reduce(jnp.maximum, [m_prev] + [part[0] for part in parts])
    alpha = jnp.exp2(m_prev - m_new)
    l_new = alpha * l_ref[...]
    acc_new = jnp.concatenate([alpha, alpha], axis=-1) * acc_ref[...]
    for m_i, l_i, pv_i in parts:
        w = jnp.exp2(m_i - m_new)
        l_new = l_new + w * l_i
        acc_new = acc_new + jnp.concatenate([w, w], axis=-1) * pv_i
    l_ref[...] = l_new
    acc_ref[...] = acc_new
    m_ref[...] = m_new

    @pl.when(step == n_steps - 1)
    def _():
        m_old = m_ref[:, 0:1]
        l_old = jnp.sum(l_ref[...], axis=-1, keepdims=True)
        s = scores(kn_ref[0], biasn_ref[...])
        m_fin = jnp.maximum(m_old, jnp.max(s, axis=-1, keepdims=True))
        a_fin = jnp.exp2(m_old - m_fin)
        p = jnp.exp2(s - m_fin)
        l_fin = a_fin * l_old + jnp.sum(p, axis=-1, keepdims=True)
        acc = a_fin * acc_ref[...] + jnp.dot(p.astype(BF16), vn_ref[0].astype(BF16), preferred_element_type=F32)
        norm = acc / l_fin
        half = SAMPLE_ROWS // 2
        o = norm[0:half, :] - lam_ref[0] * norm[half:, :]
        o_ref[0] = _subln_gate(o, gate_ref[0], sg_ref).astype(BF16)


def _sample_attention(page_table, lam, qm, cache_k, cache_v, k_new, v_new, gate_r, subln_g, *, dec_seq):
    dec_batch, n_pages = page_table.shape
    g = PAGES_PER_STEP
    n_steps = n_pages // g
    rows_new = dec_seq * DA_HEADS
    row_head = (jnp.arange(SAMPLE_ROWS) // dec_seq) % DA_HEADS
    col = jnp.arange(PAGE_ROWS)
    bias = jnp.where((col % DA_HEADS)[None, :] == row_head[:, None], 0.0, NEG_BIG).astype(F32)
    col_n = jnp.arange(rows_new)
    valid_n = ((col_n % DA_HEADS)[None, :] == row_head[:, None]) & (
        (col_n // DA_HEADS)[None, :] <= (jnp.arange(SAMPLE_ROWS) % dec_seq)[:, None])
    bias_n = jnp.where(valid_n, 0.0, NEG_BIG).astype(F32)

    def page_spec(i):
        return pl.BlockSpec((1, 1, PAGE_SIZE, DA_HEADS, DA_V_DIM),
                            lambda b, p, pt: (0, pt[b, p * g + i], 0, 0, 0))

    per_b = lambda rows, cols: pl.BlockSpec((1, rows, cols), lambda b, p, pt: (b, 0, 0))
    const = lambda shape: pl.BlockSpec(shape, lambda b, p, pt: (0, 0))
    grid_spec = pltpu.PrefetchScalarGridSpec(
        num_scalar_prefetch=1,
        grid=(dec_batch, n_steps),
        in_specs=[pl.BlockSpec(memory_space=pltpu.SMEM), per_b(SAMPLE_ROWS, DA_V_DIM)]
        + [page_spec(i) for i in range(g)] + [page_spec(i) for i in range(g)]
        + [const((SAMPLE_ROWS, PAGE_ROWS)), per_b(rows_new, DA_V_DIM), per_b(rows_new, DA_V_DIM),
           const((SAMPLE_ROWS, rows_new)), per_b(rows_new, DA_V_DIM), const((1, DA_V_DIM))],
        out_specs=per_b(rows_new, DA_V_DIM),
        scratch_shapes=[pltpu.VMEM((SAMPLE_ROWS, LANES), F32), pltpu.VMEM((SAMPLE_ROWS, LANES), F32),
                        pltpu.VMEM((SAMPLE_ROWS, DA_V_DIM), F32)],
    )
    return pl.pallas_call(
        functools.partial(_sample_attn_kernel, n_steps=n_steps),
        out_shape=jax.ShapeDtypeStruct((dec_batch, rows_new, DA_V_DIM), BF16),
        grid_spec=grid_spec,
        compiler_params=pltpu.CompilerParams(
            dimension_semantics=("arbitrary", "arbitrary"), vmem_limit_bytes=VMEM_LIMIT),
        name="paged_diff_attn",
    )(page_table, lam, qm, *([cache_k] * g), *([cache_v] * g), bias, k_new, v_new, bias_n, gate_r, subln_g)


def _tile_cumsum(x, rowi):
    for s in (1, 2, 4):
        x = x + jnp.where(rowi >= s, pltpu.roll(x, s, 0), 0.0)
    return x


def _pair_levels(length, nheads):
    n = length * nheads
    ri = lax.broadcasted_iota(jnp.int32, (n, n), 0)
    ci = lax.broadcasted_iota(jnp.int32, (n, n), 1)
    lvl = jnp.zeros((n, n), jnp.int32)
    shift = 0
    while (1 << shift) < length:
        lvl = lvl + jnp.where((ri >> shift) != (ci >> shift), 1, 0)
        shift += 1
    return jnp.where(((ri >> shift) != (ci >> shift)) | (ci > ri), -1, lvl)


def _hgrn_chunk(q, k, v, lf, states, levels):
    nheads = len(states)
    length = q[0].shape[0]
    nt = length // SUBLANES
    rowi = lax.broadcasted_iota(jnp.int32, (SUBLANES, LANES), 0)
    tile = lambda a, r: a[r * SUBLANES:(r + 1) * SUBLANES, :]
    stack = lambda tiles: jnp.concatenate(tiles, axis=0) if len(tiles) > 1 else tiles[0]
    bcast_row = lambda a, r, n: jnp.broadcast_to(a[r:r + 1, :], (n, LANES))

    b_tiles, b, o_inter, st_new = [], [], [], []
    for h in range(nheads):
        tiles, carry = [], None
        for r in range(nt):
            c = _tile_cumsum(tile(lf[h], r), rowi)
            if carry is not None:
                c = c + carry
            carry = c[SUBLANES - 1:SUBLANES, :]
            tiles.append(c)
        b_tiles.append(tiles)
        b.append(stack(tiles))
        b_last = carry
        qe = (q[h] * jnp.exp(b[h])).astype(BF16)
        o_inter.append(lax.dot_general(qe, states[h].astype(BF16), NT_DIMS, preferred_element_type=F32))
        kd = (k[h] * jnp.exp(b_last - b[h])).astype(BF16)
        st_new.append(states[h] * jnp.exp(b_last)
                      + lax.dot_general(v[h].astype(BF16), kd, TN_DIMS, preferred_element_type=F32))

    a_mat = jnp.where(levels == 0,
                      lax.dot_general(stack(q).astype(BF16), stack(k).astype(BF16), NT_DIMS,
                                      preferred_element_type=F32), 0.0)
    level, half = 1, 1
    while half < length:
        blk = 2 * half
        q_s, k_s = [], []
        for h in range(nheads):
            if half == 1:
                ref_b = stack([jnp.where(rowi % 2 == 1, pltpu.roll(bt, 1, 0), bt) for bt in b_tiles[h]])
            elif blk < SUBLANES:
                ref_b = stack([jnp.where(rowi < blk, bcast_row(bt, half - 1, SUBLANES),
                                         bcast_row(bt, blk + half - 1, SUBLANES)) for bt in b_tiles[h]])
            else:
                ref_b = stack([bcast_row(b[h], b0 + half - 1, blk) for b0 in range(0, length, blk)])
            e = jnp.exp2(jnp.abs(b[h] - ref_b) * -LOG2E)
            q_s.append(q[h] * e)
            k_s.append(k[h] * e)
        s = lax.dot_general(stack(q_s).astype(BF16), stack(k_s).astype(BF16), NT_DIMS,
                            preferred_element_type=F32)
        a_mat = jnp.where(levels == level, s, a_mat)
        level, half = level + 1, blk
    o_intra = jnp.dot(a_mat.astype(BF16), stack(v).astype(BF16), preferred_element_type=F32)
    outs = [o_inter[h] + o_intra[h * length:(h + 1) * length, :] for h in range(nheads)]
    return outs, st_new


HEADS_PER_STEP = 16
STACKED_HEADS = 4


def _hgrn_prompt_kernel(q_ref, k_ref, lf_ref, v_ref, o_ref, s_ref, st_ref, *, chunk, nchunks, nblocks):
    tb = pl.program_id(2)

    @pl.when(tb == 0)
    def _():
        st_ref[...] = jnp.zeros(st_ref.shape, F32)

    levels = _pair_levels(chunk, STACKED_HEADS)
    head_cols = [slice(hh * HG_DK, (hh + 1) * HG_DK) for hh in range(HEADS_PER_STEP)]

    def body(ci, carry):
        rows = pl.ds(pl.multiple_of(ci * chunk, chunk), chunk)
        for g0 in range(0, HEADS_PER_STEP, STACKED_HEADS):
            group = range(g0, g0 + STACKED_HEADS)
            per_head = lambda ref: [ref[rows, head_cols[hh]] for hh in group]
            outs, st_new = _hgrn_chunk(per_head(q_ref), per_head(k_ref), per_head(v_ref), per_head(lf_ref),
                                       [st_ref[hh] for hh in group], levels)
            for idx, hh in enumerate(group):
                o_ref[rows, head_cols[hh]] = outs[idx]
                st_ref[hh] = st_new[idx]
        return carry

    lax.fori_loop(0, nchunks, body, 0)

    @pl.when(tb == nblocks - 1)
    def _():
        for hh in range(HEADS_PER_STEP):
            s_ref[hh] = st_ref[hh].T


def _hgrn_prompt(q, k, lf, v, *, batch, seq, tblock, chunk):
    nblocks = seq // tblock
    width = HEADS_PER_STEP * HG_DK
    tok = pl.BlockSpec((tblock, width), lambda b, h, t: (b * nblocks + t, h))
    return pl.pallas_call(
        functools.partial(_hgrn_prompt_kernel, chunk=chunk, nchunks=tblock // chunk, nblocks=nblocks),
        out_shape=(jax.ShapeDtypeStruct((batch * seq, D_MODEL), F32),
                   jax.ShapeDtypeStruct((batch, HG_HEADS, HG_DK, HG_DK), F32)),
        grid=(batch, HG_HEADS // HEADS_PER_STEP, nblocks),
        in_specs=[tok, tok, tok, tok],
        out_specs=(tok, pl.BlockSpec((None, HEADS_PER_STEP, HG_DK, HG_DK), lambda b, h, t: (b, h, 0, 0))),
        scratch_shapes=[pltpu.VMEM((HEADS_PER_STEP, HG_DK, HG_DK), F32)],
        compiler_params=pltpu.CompilerParams(
            dimension_semantics=("arbitrary", "arbitrary", "arbitrary"), vmem_limit_bytes=VMEM_LIMIT),
        name="hgrn_prompt_scan",
    )(q, k, lf, v)


def _hgrn_sample_kernel(q_ref, k_ref, lf_ref, v_ref, s0_ref, o_ref, s1_ref):
    levels = _pair_levels(SUBLANES, STACKED_HEADS)

    for g0 in range(0, HG_HEADS, STACKED_HEADS):
        heads = range(g0, g0 + STACKED_HEADS)
        per_head = lambda ref: [ref[0, h] for h in heads]
        outs, st_new = _hgrn_chunk(per_head(q_ref), per_head(k_ref), per_head(v_ref), per_head(lf_ref),
                                   [s0_ref[h].T for h in heads], levels)
        for idx, h in enumerate(heads):
            o_ref[0, h] = outs[idx]
            s1_ref[h] = st_new[idx].T


def _hgrn_sample(q, k, lf, v, state):
    dec_batch = q.shape[0]
    tok = pl.BlockSpec((1, HG_HEADS, SUBLANES, HG_DK), lambda b: (b, 0, 0, 0))
    st = pl.BlockSpec((None, HG_HEADS, HG_DK, HG_DK), lambda b: (b, 0, 0, 0))
    return pl.pallas_call(
        _hgrn_sample_kernel,
        out_shape=(jax.ShapeDtypeStruct(q.shape, F32), jax.ShapeDtypeStruct(state.shape, F32)),
        grid=(dec_batch,),
        in_specs=[tok, tok, tok, tok, st],
        out_specs=(tok, st),
        compiler_params=pltpu.CompilerParams(
            dimension_semantics=("arbitrary",), vmem_limit_bytes=VMEM_LIMIT),
        name="hgrn_sample_step",
    )(q, k, lf, v, state)


def _rope_tables(pos):
    inv = jnp.power(ROPE_THETA, -jnp.arange(0, ROPE_DIM, 2, dtype=F32) / ROPE_DIM)
    ang = pos.astype(F32)[:, None] * inv[None, :]
    cos, sin = jnp.cos(ang), jnp.sin(ang)
    n = pos.shape[0]
    pad1 = jnp.ones((n, LANES - ROPE_DIM), F32)
    pad0 = jnp.zeros((n, LANES - ROPE_DIM), F32)
    zero_h = jnp.zeros_like(sin)
    cos_t = jnp.concatenate([cos, cos, pad1], axis=1)
    sa_t = jnp.concatenate([zero_h, sin, pad0], axis=1)
    sb_t = jnp.concatenate([-sin, zero_h, pad0], axis=1)
    return cos_t, sa_t, sb_t


def kernel(x_prompt, x_sample, cache_k, cache_v, state_hgrn, page_table, attn_norm, attn_w_in, attn_q_norm,
           attn_k_norm, attn_lambda, attn_subln, attn_w_out, hgrn_norm, hgrn_w_in, hgrn_lower_bounds,
           hgrn_out_norm, hgrn_w_out):
    batch, seq, _ = x_prompt.shape
    dec_batch, dec_seq, _ = x_sample.shape
    mp, ms = batch * seq, dec_batch * dec_seq
    xp = x_prompt.reshape(mp, D_MODEL)
    xs = x_sample.reshape(ms, D_MODEL)

    lq1, lk1, lq2, lk2 = attn_lambda[0].astype(F32)
    lam = (jnp.exp(jnp.sum(lq1 * lk1)) - jnp.exp(jnp.sum(lq2 * lk2)) + LAMBDA_INIT_0).reshape(1)
    w_out = attn_w_out[0].astype(BF16)
    norm_g = attn_norm[0].reshape(1, D_MODEL)
    subln_g = attn_subln[0].reshape(1, DA_V_DIM)

    tabs_p = _rope_tables(jnp.arange(seq))
    tabs_s = _rope_tables(jnp.tile(PAST_LEN + jnp.arange(dec_seq), dec_batch))

    q_s, k_s, _, v_s, _, g_s, w_in = _attn_proj(xs, norm_g, attn_w_in[0], attn_q_norm[0], attn_k_norm[0], *tabs_s,
                                                tm=ms, tn=1024)
    q_p, k_p, kb_p, v_p, vb_p, g_p = _attn_proj(xp, norm_g, w_in, attn_q_norm[0], attn_k_norm[0], *tabs_p,
                                                tm=1024, tn=512)
    og_p = _flash_attention(lam, q_p, kb_p, vb_p, g_p, subln_g, batch=batch, seq=seq, tq=1024)
    xp = _out_proj(og_p, w_out, xp, tm=512)

    q5 = q_s.reshape(dec_batch, dec_seq, DA_HEADS, 2, DA_HEAD_DIM).transpose(0, 3, 2, 1, 4)
    zeros = jnp.zeros_like(q5[:, 0])
    qm = jnp.stack([jnp.concatenate([q5[:, 0], zeros], axis=-1),
                    jnp.concatenate([zeros, q5[:, 1]], axis=-1)], axis=1)
    qm = qm.reshape(dec_batch, SAMPLE_ROWS, DA_V_DIM)
    rows_new = dec_seq * DA_HEADS
    gate_r = g_s.reshape(dec_batch, dec_seq, DA_HEADS, DA_V_DIM).transpose(0, 2, 1, 3)
    gate_r = gate_r.reshape(dec_batch, rows_new, DA_V_DIM)
    og_s = _sample_attention(page_table, lam, qm, cache_k, cache_v,
                             k_s.reshape(dec_batch, rows_new, DA_V_DIM),
                             v_s.reshape(dec_batch, rows_new, DA_V_DIM), gate_r, subln_g, dec_seq=dec_seq)
    og_s = og_s.reshape(dec_batch, DA_HEADS, dec_seq, DA_V_DIM).transpose(0, 2, 1, 3).reshape(ms, D_MODEL)
    xs = _out_proj(og_s, w_out, xs, tm=ms)

    k_prompt = k_p.reshape(1, batch, seq, DA_HEADS, 2 * DA_HEAD_DIM)
    v_prompt = v_p.reshape(1, batch, seq, DA_HEADS, DA_V_DIM)
    k_sample = k_s.reshape(1, dec_batch, dec_seq, DA_HEADS, 2 * DA_HEAD_DIM)
    v_sample = v_s.reshape(1, dec_batch, dec_seq, DA_HEADS, DA_V_DIM)

    pr = jax.nn.softmax(hgrn_lower_bounds.astype(F32), axis=0)
    lb = (jnp.cumsum(pr, axis=0)[1] - pr[0]).reshape(1, D_MODEL)
    hw_out = hgrn_w_out[0].astype(BF16)
    hnorm_g = hgrn_norm[0].reshape(1, D_MODEL)
    hout_g = hgrn_out_norm[0].reshape(1, D_MODEL)

    sq, sk, slf, sv, sg, hw_in = _hgrn_proj(xs, hnorm_g, hgrn_w_in[0], lb, tm=ms, tn=1024)
    hq, hk, hlf, hv, hg = _hgrn_proj(xp, hnorm_g, hw_in, lb, tm=1024, tn=512)
    ho, s_prompt = _hgrn_prompt(hq, hk, hlf, hv, batch=batch, seq=seq, tblock=512, chunk=64)
    xp = _hgrn_out(ho, hg, hout_g, hw_out, xp, tm=512)

    def to_heads(a):
        a = a.reshape(dec_batch, dec_seq, HG_HEADS, HG_DK).transpose(0, 2, 1, 3)
        return jnp.pad(a, ((0, 0), (0, 0), (0, SUBLANES - dec_seq), (0, 0)))

    so, s_sample = _hgrn_sample(to_heads(sq), to_heads(sk), to_heads(slf), to_heads(sv), state_hgrn[0])
    so = so[:, :, :dec_seq].transpose(0, 2, 1, 3).reshape(ms, D_MODEL)
    xs = _hgrn_out(so, sg, hout_g, hw_out, xs, tm=ms)

    return (xp.reshape(batch, seq, D_MODEL), xs.reshape(dec_batch, dec_seq, D_MODEL),
            k_prompt, v_prompt, k_sample, v_sample, s_prompt[None], s_sample[None])
```

```python
import functools
import math

import jax
import jax.numpy as jnp
from jax import lax
from jax.experimental import pallas as pl
from jax.experimental.pallas import tpu as pltpu

F32 = jnp.float32
BF16 = jnp.bfloat16

D_MODEL = 2048
PAST_LEN = 8192
PAGE_SIZE = 128
DA_HEADS = 8
DA_HEAD_DIM = 128
DA_V_DIM = 256
ROPE_DIM = 32
ROPE_THETA = 500000.0
HG_HEADS = 16
HG_DK = 128
EPS = 1e-6
LAMBDA_INIT_0 = 0.8 - 0.6 * math.exp(-0.3 * 0)

LANES = 128
SUBLANES = 8
VMEM_LIMIT = 56 * 1024 * 1024
NEG_BIG = -1e30
LOG2E = math.log2(math.e)
EXP2_SCALE = DA_HEAD_DIM ** -0.5 * LOG2E

NT_DIMS = (((1,), (1,)), ((), ()))
TN_DIMS = (((0,), (0,)), ((), ()))


def _silu(x):
    return x * (1.0 / (1.0 + jnp.exp(-x)))


def _sigmoid(x):
    return 1.0 / (1.0 + jnp.exp(-x))


def _group_map(group, nblk):
    return lambda i, j: (i, jnp.clip(j - group * nblk, 0, nblk - 1))


def _norm_rows(x_ref, g_ref, xn_ref):
    x = x_ref[...]
    ms = jnp.mean(x * x, axis=-1, keepdims=True)
    xn_ref[...] = (x * lax.rsqrt(ms + EPS) * g_ref[...]).astype(BF16)


ROW_SPLITS = 8
MIN_SLAB_ROWS = 128


def _paired_slabs(xn_ref, wa_ref, wb_ref, emit_a, emit_b):
    rows = min(xn_ref.shape[0], max(MIN_SLAB_ROWS, xn_ref.shape[0] // ROW_SPLITS))
    for r in range(xn_ref.shape[0] // rows):
        sl = slice(r * rows, (r + 1) * rows)
        emit_a(sl, jnp.dot(xn_ref[sl, :], wa_ref[...], preferred_element_type=F32))
        emit_b(sl, jnp.dot(xn_ref[sl, :], wb_ref[...], preferred_element_type=F32))


def _proj_weights(wa_ref, wb_ref, rest):
    if len(rest) == 1:
        return wa_ref, wb_ref, rest[0]
    wa_out, wb_out, xn_ref = rest
    wa_out[...] = wa_ref[...].astype(BF16)
    wb_out[...] = wb_ref[...].astype(BF16)
    return wa_out, wb_out, xn_ref


def _attn_proj_kernel(x_ref, g_ref, wa_ref, wb_ref, qg_ref, kg_ref, cos_ref, sa_ref, sb_ref,
                      q_ref, k_ref, kb_ref, v_ref, vb_ref, gate_ref, *rest, tn, nblk):
    wa_ref, wb_ref, xn_ref = _proj_weights(wa_ref, wb_ref, rest)
    j = pl.program_id(1)

    @pl.when(j == 0)
    def _():
        _norm_rows(x_ref, g_ref, xn_ref)

    def qk_post(acc, sl, gain_ref, emit):
        for idx in range(tn // LANES):
            c = idx % 2
            a = acc[:, idx * LANES:(idx + 1) * LANES]
            ms = jnp.mean(a * a, axis=-1, keepdims=True)
            y = a * lax.rsqrt(ms + EPS) * gain_ref[c:c + 1, :]
            y = (y * cos_ref[sl, :] + pltpu.roll(y, ROPE_DIM // 2, 1) * sa_ref[sl, :]
                 + pltpu.roll(y, LANES - ROPE_DIM // 2, 1) * sb_ref[sl, :])
            emit(slice(idx * LANES, (idx + 1) * LANES), y)

    @pl.when(j < nblk)
    def _():
        def emit_q(sl, acc):
            def put(cols, y):
                q_ref[sl, cols] = (y * EXP2_SCALE).astype(BF16)
            qk_post(acc, sl, qg_ref, put)

        def emit_v(sl, acc):
            v_ref[sl, :] = acc
            vb_ref[sl, :] = acc.astype(BF16)
        _paired_slabs(xn_ref, wa_ref, wb_ref, emit_q, emit_v)

    @pl.when(j >= nblk)
    def _():
        def emit_k(sl, acc):
            def put(cols, y):
                k_ref[sl, cols] = y
                kb_ref[sl, cols] = y.astype(BF16)
            qk_post(acc, sl, kg_ref, put)

        def emit_gate(sl, acc):
            gate_ref[sl, :] = acc
        _paired_slabs(xn_ref, wa_ref, wb_ref, emit_k, emit_gate)


def _proj_weight_specs(weights, tn, nblk):
    half = pl.BlockSpec((D_MODEL, tn), lambda i, j: (0, j))
    if len(weights) == 2:
        return list(weights), [half, half], (), ()
    (w,) = weights
    second_half = pl.BlockSpec((D_MODEL, tn), lambda i, j: (0, 2 * nblk + j))
    half_sd = jax.ShapeDtypeStruct((D_MODEL, w.shape[1] // 2), BF16)
    return [w, w], [half, second_half], (half_sd, half_sd), (half, half)


def _attn_proj(x, norm_g, weights, q_g, k_g, cos_t, sa_t, sb_t, *, tm, tn):
    m = x.shape[0]
    nblk = D_MODEL // tn
    w_args, w_specs, w_out_sd, w_out_specs = _proj_weight_specs(weights, tn, nblk)
    assert not w_out_sd or m == tm
    tab_blocks = cos_t.shape[0] // tm
    tab_spec = pl.BlockSpec((tm, LANES), lambda i, j: (i % tab_blocks, 0))
    small = lambda shape: pl.BlockSpec(shape, lambda i, j: (0, 0))
    out_sd = lambda dt: jax.ShapeDtypeStruct((m, D_MODEL), dt)
    out_spec = lambda g: pl.BlockSpec((tm, tn), _group_map(g, nblk))
    return pl.pallas_call(
        functools.partial(_attn_proj_kernel, tn=tn, nblk=nblk),
        out_shape=(out_sd(BF16), out_sd(F32), out_sd(BF16), out_sd(F32), out_sd(BF16), out_sd(F32)) + w_out_sd,
        grid=(m // tm, 2 * nblk),
        in_specs=[pl.BlockSpec((tm, D_MODEL), lambda i, j: (i, 0)), small((1, D_MODEL))] + w_specs
        + [small((2, LANES)), small((2, LANES)), tab_spec, tab_spec, tab_spec],
        out_specs=(out_spec(0), out_spec(1), out_spec(1), out_spec(0), out_spec(0), out_spec(1)) + w_out_specs,
        scratch_shapes=[pltpu.VMEM((tm, D_MODEL), BF16)],
        compiler_params=pltpu.CompilerParams(
            dimension_semantics=("arbitrary", "arbitrary"), vmem_limit_bytes=VMEM_LIMIT),
        name="attn_proj",
    )(x, norm_g, *w_args, q_g, k_g, cos_t, sa_t, sb_t)


def _hgrn_proj_kernel(x_ref, g_ref, wa_ref, wb_ref, lb_ref, q_ref, k_ref, lf_ref, i_ref, gate_ref, *rest, nblk):
    wa_ref, wb_ref, xn_ref = _proj_weights(wa_ref, wb_ref, rest)
    j = pl.program_id(1)

    @pl.when(j == 0)
    def _():
        _norm_rows(x_ref, g_ref, xn_ref)

    @pl.when(j < nblk)
    def _():
        def emit_q(sl, acc):
            q_ref[sl, :] = _silu(acc)

        def emit_i(sl, acc):
            i_ref[sl, :] = acc
        _paired_slabs(xn_ref, wa_ref, wb_ref, emit_q, emit_i)

    @pl.when(j >= nblk)
    def _():
        def emit_f(sl, acc):
            lb = lb_ref[...]
            fg = lb + (1.0 - lb) * _sigmoid(acc)
            k_ref[sl, :] = 1.0 - fg
            lf_ref[sl, :] = jnp.log(fg)

        def emit_gate(sl, acc):
            gate_ref[sl, :] = acc
        _paired_slabs(xn_ref, wa_ref, wb_ref, emit_f, emit_gate)


def _hgrn_proj(x, norm_g, weights, lb, *, tm, tn):
    m = x.shape[0]
    nblk = D_MODEL // tn
    w_args, w_specs, w_out_sd, w_out_specs = _proj_weight_specs(weights, tn, nblk)
    assert not w_out_sd or m == tm
    out_sd = jax.ShapeDtypeStruct((m, D_MODEL), F32)
    out_spec = lambda g: pl.BlockSpec((tm, tn), _group_map(g, nblk))
    return pl.pallas_call(
        functools.partial(_hgrn_proj_kernel, nblk=nblk),
        out_shape=(out_sd,) * 5 + w_out_sd,
        grid=(m // tm, 2 * nblk),
        in_specs=[pl.BlockSpec((tm, D_MODEL), lambda i, j: (i, 0)), pl.BlockSpec((1, D_MODEL), lambda i, j: (0, 0))]
        + w_specs + [pl.BlockSpec((1, tn), lambda i, j: (0, jnp.clip(j - nblk, 0, nblk - 1)))],
        out_specs=(out_spec(0), out_spec(1), out_spec(1), out_spec(0), out_spec(1)) + w_out_specs,
        scratch_shapes=[pltpu.VMEM((tm, D_MODEL), BF16)],
        compiler_params=pltpu.CompilerParams(
            dimension_semantics=("arbitrary", "arbitrary"), vmem_limit_bytes=VMEM_LIMIT),
        name="hgrn_proj",
    )(x, norm_g, *w_args, lb)


def _out_proj_kernel(a_ref, w_ref, x_ref, y_ref):
    y_ref[...] = x_ref[...] + jnp.dot(a_ref[...], w_ref[...], preferred_element_type=F32)


def _out_proj(a_bf, w_bf, x, *, tm):
    m = x.shape[0]
    row = pl.BlockSpec((tm, D_MODEL), lambda i: (i, 0))
    return pl.pallas_call(
        _out_proj_kernel,
        out_shape=jax.ShapeDtypeStruct((m, D_MODEL), F32),
        grid=(m // tm,),
        in_specs=[row, pl.BlockSpec((D_MODEL, D_MODEL), lambda i: (0, 0), pipeline_mode=pl.Buffered(1)), row],
        out_specs=row,
        compiler_params=pltpu.CompilerParams(
            dimension_semantics=("arbitrary",), vmem_limit_bytes=VMEM_LIMIT),
        name="attn_out_proj",
    )(a_bf, w_bf, x)


def _hgrn_out_kernel(o_ref, gate_ref, g_ref, w_ref, x_ref, y_ref):
    o = o_ref[...]
    ms = jnp.mean(o * o, axis=-1, keepdims=True)
    a = (o * lax.rsqrt(ms + EPS) * g_ref[...]) * _silu(gate_ref[...])
    y_ref[...] = x_ref[...] + jnp.dot(a.astype(BF16), w_ref[...], preferred_element_type=F32)


def _hgrn_out(o, gate, out_g, w_bf, x, *, tm):
    m = x.shape[0]
    row = pl.BlockSpec((tm, D_MODEL), lambda i: (i, 0))
    return pl.pallas_call(
        _hgrn_out_kernel,
        out_shape=jax.ShapeDtypeStruct((m, D_MODEL), F32),
        grid=(m // tm,),
        in_specs=[row, row, pl.BlockSpec((1, D_MODEL), lambda i: (0, 0)),
                  pl.BlockSpec((D_MODEL, D_MODEL), lambda i: (0, 0), pipeline_mode=pl.Buffered(1)), row],
        out_specs=row,
        compiler_params=pltpu.CompilerParams(
            dimension_semantics=("arbitrary",), vmem_limit_bytes=VMEM_LIMIT),
        name="hgrn_out_proj",
    )(o, gate, out_g, w_bf, x)


def _lane_tiles(x):
    return [x[:, j * LANES:(j + 1) * LANES] for j in range(x.shape[1] // LANES)]


def _subln_gate(o, gate, sg_ref):
    ms = jnp.mean(o * o, axis=-1, keepdims=True)
    o = (o * lax.rsqrt(ms + EPS) * sg_ref[...]) * (1.0 - LAMBDA_INIT_0)
    return o * _silu(gate)


def _flash_kernel(lam_ref, q_ref, k_ref, v_ref, gate_ref, sg_ref, o_ref, m0, m1, l0, l1, a0, a1, *, tq):
    m_refs, l_refs, acc_refs = (m0, m1), (l0, l1), (a0, a1)
    qi = pl.program_id(2)
    for c in range(2):
        m_refs[c][...] = jnp.full(m_refs[c].shape, NEG_BIG, F32)
        l_refs[c][...] = jnp.zeros(l_refs[c].shape, F32)
        acc_refs[c][...] = jnp.zeros(acc_refs[c].shape, F32)

    def step(ki, masked):
        kv_rows = pl.ds(pl.multiple_of(ki * tq, tq), tq)
        v_bf = v_ref[kv_rows, :]
        comps = [slice(c * LANES, (c + 1) * LANES) for c in range(2)]
        scores = [lax.dot_general(q_ref[:, comp], k_ref[kv_rows, comp], NT_DIMS, preferred_element_type=F32)
                  for comp in comps]
        probs, alphas = [], []
        for c in range(2):
            s = scores[c]
            if masked:
                r = lax.broadcasted_iota(jnp.int32, (tq, tq), 0)
                col = lax.broadcasted_iota(jnp.int32, (tq, tq), 1)
                s = jnp.where(col <= r, s, NEG_BIG)
            tiles = _lane_tiles(s)
            m_prev = m_refs[c][...]
            m_new = jnp.maximum(m_prev, jnp.max(functools.reduce(jnp.maximum, tiles), axis=-1, keepdims=True))
            alpha = jnp.exp2(m_prev - m_new)
            p_tiles = [jnp.exp2(t - m_new) for t in tiles]
            l_refs[c][...] = alpha * l_refs[c][...] + functools.reduce(jnp.add, p_tiles)
            m_refs[c][...] = m_new
            probs.append(jnp.concatenate(p_tiles, axis=-1).astype(BF16))
            alphas.append(alpha)
        pv = jnp.dot(jnp.concatenate(probs, axis=0), v_bf, preferred_element_type=F32)
        for c in range(2):
            acc_refs[c][...] = (jnp.concatenate([alphas[c], alphas[c]], axis=-1) * acc_refs[c][...]
                                + pv[c * tq:(c + 1) * tq, :])

    def body(ki, carry):
        step(ki, False)
        return carry

    lax.fori_loop(0, qi, body, 0)
    step(qi, True)

    norm = [acc_refs[c][...] / jnp.sum(l_refs[c][...], axis=-1, keepdims=True) for c in range(2)]
    o = norm[0] - lam_ref[0] * norm[1]
    o_ref[...] = _subln_gate(o, gate_ref[...], sg_ref).astype(BF16)


def _flash_attention(lam, q_bf, k_bf, v_bf, gate, subln_g, *, batch, seq, tq):
    nq = seq // tq
    qmap = lambda b, h, qi: (b * nq + qi, h)
    kvmap = lambda b, h, qi: (b, h)
    stat = pltpu.VMEM((tq, LANES), F32)
    acc = pltpu.VMEM((tq, DA_V_DIM), F32)
    return pl.pallas_call(
        functools.partial(_flash_kernel, tq=tq),
        out_shape=jax.ShapeDtypeStruct((batch * seq, DA_HEADS * DA_V_DIM), BF16),
        grid=(batch, DA_HEADS, nq),
        in_specs=[pl.BlockSpec(memory_space=pltpu.SMEM),
                  pl.BlockSpec((tq, DA_V_DIM), qmap),
                  pl.BlockSpec((seq, DA_V_DIM), kvmap),
                  pl.BlockSpec((seq, DA_V_DIM), kvmap),
                  pl.BlockSpec((tq, DA_V_DIM), qmap),
                  pl.BlockSpec((1, DA_V_DIM), lambda b, h, qi: (0, 0))],
        out_specs=pl.BlockSpec((tq, DA_V_DIM), qmap),
        scratch_shapes=[stat, stat, stat, stat, acc, acc],
        compiler_params=pltpu.CompilerParams(
            dimension_semantics=("arbitrary", "arbitrary", "arbitrary"), vmem_limit_bytes=VMEM_LIMIT),
        name="flash_diff_attn",
    )(lam, q_bf, k_bf, v_bf, gate, subln_g)


SAMPLE_ROWS = 64
PAGE_ROWS = PAGE_SIZE * DA_HEADS
PAGES_PER_STEP = 8


def _sample_attn_kernel(pt_ref, lam_ref, qm_ref, *rest, n_steps):
    g = PAGES_PER_STEP
    k_refs, v_refs = rest[:g], rest[g:2 * g]
    bias_ref, kn_ref, vn_ref, biasn_ref, gate_ref, sg_ref, o_ref, m_ref, l_ref, acc_ref = rest[2 * g:]
    step = pl.program_id(1)

    @pl.when(step == 0)
    def _():
        m_ref[...] = jnp.full(m_ref.shape, NEG_BIG, F32)
        l_ref[...] = jnp.zeros(l_ref.shape, F32)
        acc_ref[...] = jnp.zeros(acc_ref.shape, F32)

    def scores(kf, bias):
        return lax.dot_general(qm_ref[0], kf.astype(BF16), NT_DIMS, preferred_element_type=F32) + bias

    flat = lambda ref: ref[0, 0].reshape(PAGE_ROWS, DA_V_DIM)
    parts = []
    for i in range(g):
        tiles = _lane_tiles(scores(flat(k_refs[i]), bias_ref[...]))
        m_i = jnp.broadcast_to(jnp.max(functools.reduce(jnp.maximum, tiles), axis=-1, keepdims=True),
                               (SAMPLE_ROWS, LANES))
        p_tiles = [jnp.exp2(t - m_i) for t in tiles]
        pv_i = jnp.dot(jnp.concatenate(p_tiles, axis=-1).astype(BF16), flat(v_refs[i]).astype(BF16),
                       preferred_element_type=F32)
        parts.append((m_i, functools.reduce(jnp.add, p_tiles), pv_i))
    m_prev = m_ref[...]
    m_new = functools.reduce(jnp.maximum, [m_prev] + [part[0] for part in parts])
    alpha = jnp.exp2(m_prev - m_new)
    l_new = alpha * l_ref[...]
    acc_new = jnp.concatenate([alpha, alpha], axis=-1) * acc_ref[...]
    for m_i, l_i, pv_i in parts:
        w = jnp.exp2(m_i - m_new)
        l_new = l_new + w * l_i
        acc_new = acc_new + jnp.concatenate([w, w], axis=-1) * pv_i
    l_ref[...] = l_new
    acc_ref[...] = acc_new
    m_ref[...] = m_new

    @pl.when(step == n_steps - 1)
    def _():
        m_old = m_ref[:, 0:1]
        l_old = jnp.sum(l_ref[...], axis=-1, keepdims=True)
        s = scores(kn_ref[0], biasn_ref[...])
        m_fin = jnp.maximum(m_old, jnp.max(s, axis=-1, keepdims=True))
        a_fin = jnp.exp2(m_old - m_fin)
        p = jnp.exp2(s - m_fin)
        l_fin = a_fin * l_old + jnp.sum(p, axis=-1, keepdims=True)
        acc = a_fin * acc_ref[...] + jnp.dot(p.astype(BF16), vn_ref[0].astype(BF16), preferred_element_type=F32)
        norm = acc / l_fin
        half = SAMPLE_ROWS // 2
        o = norm[0:half, :] - lam_ref[0] * norm[half:, :]
        o_ref[0] = _subln_gate(o, gate_ref[0], sg_ref).astype(BF16)


def _sample_attention(page_table, lam, qm, cache_k, cache_v, k_new, v_new, gate_r, subln_g, *, dec_seq):
    dec_batch, n_pages = page_table.shape
    g = PAGES_PER_STEP
    n_steps = n_pages // g
    rows_new = dec_seq * DA_HEADS
    row_head = (jnp.arange(SAMPLE_ROWS) // dec_seq) % DA_HEADS
    col = jnp.arange(PAGE_ROWS)
    bias = jnp.where((col % DA_HEADS)[None, :] == row_head[:, None], 0.0, NEG_BIG).astype(F32)
    col_n = jnp.arange(rows_new)
    valid_n = ((col_n % DA_HEADS)[None, :] == row_head[:, None]) & (
        (col_n // DA_HEADS)[None, :] <= (jnp.arange(SAMPLE_ROWS) % dec_seq)[:, None])
    bias_n = jnp.where(valid_n, 0.0, NEG_BIG).astype(F32)

    def page_spec(i):
        return pl.BlockSpec((1, 1, PAGE_SIZE, DA_HEADS, DA_V_DIM),
                            lambda b, p, pt: (0, pt[b, p * g + i], 0, 0, 0))

    per_b = lambda rows, cols: pl.BlockSpec((1, rows, cols), lambda b, p, pt: (b, 0, 0))
    const = lambda shape: pl.BlockSpec(shape, lambda b, p, pt: (0, 0))
    grid_spec = pltpu.PrefetchScalarGridSpec(
        num_scalar_prefetch=1,
        grid=(dec_batch, n_steps),
        in_specs=[pl.BlockSpec(memory_space=pltpu.SMEM), per_b(SAMPLE_ROWS, DA_V_DIM)]
        + [page_spec(i) for i in range(g)] + [page_spec(i) for i in range(g)]
        + [const((SAMPLE_ROWS, PAGE_ROWS)), per_b(rows_new, DA_V_DIM), per_b(rows_new, DA_V_DIM),
           const((SAMPLE_ROWS, rows_new)), per_b(rows_new, DA_V_DIM), const((1, DA_V_DIM))],
        out_specs=per_b(rows_new, DA_V_DIM),
        scratch_shapes=[pltpu.VMEM((SAMPLE_ROWS, LANES), F32), pltpu.VMEM((SAMPLE_ROWS, LANES), F32),
                        pltpu.VMEM((SAMPLE_ROWS, DA_V_DIM), F32)],
    )
    return pl.pallas_call(
        functools.partial(_sample_attn_kernel, n_steps=n_steps),
        out_shape=jax.ShapeDtypeStruct((dec_batch, rows_new, DA_V_DIM), BF16),
        grid_spec=grid_spec,
        compiler_params=pltpu.CompilerParams(
            dimension_semantics=("arbitrary", "arbitrary"), vmem_limit_bytes=VMEM_LIMIT),
        name="paged_diff_attn",
    )(page_table, lam, qm, *([cache_k] * g), *([cache_v] * g), bias, k_new, v_new, bias_n, gate_r, subln_g)


def _tile_cumsum(x, rowi):
    for s in (1, 2, 4):
        x = x + jnp.where(rowi >= s, pltpu.roll(x, s, 0), 0.0)
    return x


def _pair_levels(length, nheads):
    n = length * nheads
    ri = lax.broadcasted_iota(jnp.int32, (n, n), 0)
    ci = lax.broadcasted_iota(jnp.int32, (n, n), 1)
    lvl = jnp.zeros((n, n), jnp.int32)
    shift = 0
    while (1 << shift) < length:
        lvl = lvl + jnp.where((ri >> shift) != (ci >> shift), 1, 0)
        shift += 1
    return jnp.where(((ri >> shift) != (ci >> shift)) | (ci > ri), -1, lvl)


def _hgrn_chunk(q, k, v, lf, states, levels):
    nheads = len(states)
    length = q[0].shape[0]
    nt = length // SUBLANES
    rowi = lax.broadcasted_iota(jnp.int32, (SUBLANES, LANES), 0)
    tile = lambda a, r: a[r * SUBLANES:(r + 1) * SUBLANES, :]
    stack = lambda tiles: jnp.concatenate(tiles, axis=0) if len(tiles) > 1 else tiles[0]
    bcast_row = lambda a, r, n: jnp.broadcast_to(a[r:r + 1, :], (n, LANES))

    b_tiles, b, o_inter, st_new = [], [], [], []
    for h in range(nheads):
        tiles, carry = [], None
        for r in range(nt):
            c = _tile_cumsum(tile(lf[h], r), rowi)
            if carry is not None:
                c = c + carry
            carry = c[SUBLANES - 1:SUBLANES, :]
            tiles.append(c)
        b_tiles.append(tiles)
        b.append(stack(tiles))
        b_last = carry
        qe = (q[h] * jnp.exp(b[h])).astype(BF16)
        o_inter.append(lax.dot_general(qe, states[h].astype(BF16), NT_DIMS, preferred_element_type=F32))
        kd = (k[h] * jnp.exp(b_last - b[h])).astype(BF16)
        st_new.append(states[h] * jnp.exp(b_last)
                      + lax.dot_general(v[h].astype(BF16), kd, TN_DIMS, preferred_element_type=F32))

    a_mat = jnp.where(levels == 0,
                      lax.dot_general(stack(q).astype(BF16), stack(k).astype(BF16), NT_DIMS,
                                      preferred_element_type=F32), 0.0)
    level, half = 1, 1
    while half < length:
        blk = 2 * half
        q_s, k_s = [], []
        for h in range(nheads):
            if half == 1:
                ref_b = stack([jnp.where(rowi % 2 == 1, pltpu.roll(bt, 1, 0), bt) for bt in b_tiles[h]])
            elif blk < SUBLANES:
                ref_b = stack([jnp.where(rowi < blk, bcast_row(bt, half - 1, SUBLANES),
                                         bcast_row(bt, blk + half - 1, SUBLANES)) for bt in b_tiles[h]])
            else:
                ref_b = stack([bcast_row(b[h], b0 + half - 1, blk) for b0 in range(0, length, blk)])
            e = jnp.exp2(jnp.abs(b[h] - ref_b) * -LOG2E)
            q_s.append(q[h] * e)
            k_s.append(k[h] * e)
        s = lax.dot_general(stack(q_s).astype(BF16), stack(k_s).astype(BF16), NT_DIMS,
                            preferred_element_type=F32)
        a_mat = jnp.where(levels == level, s, a_mat)
        level, half = level + 1, blk
    o_intra = jnp.dot(a_mat.astype(BF16), stack(v).astype(BF16), preferred_element_type=F32)
    outs = [o_inter[h] + o_intra[h * length:(h + 1) * length, :] for h in range(nheads)]
    return outs, st_new


HEADS_PER_STEP = 16
STACKED_HEADS = 4


def _hgrn_prompt_kernel(q_ref, k_ref, lf_ref, v_ref, o_ref, s_ref, st_ref, *, chunk, nchunks, nblocks):
    tb = pl.program_id(2)

    @pl.when(tb == 0)
    def _():
        st_ref[...] = jnp.zeros(st_ref.shape, F32)

    levels = _pair_levels(chunk, STACKED_HEADS)
    head_cols = [slice(hh * HG_DK, (hh + 1) * HG_DK) for hh in range(HEADS_PER_STEP)]

    def body(ci, carry):
        rows = pl.ds(pl.multiple_of(ci * chunk, chunk), chunk)
        for g0 in range(0, HEADS_PER_STEP, STACKED_HEADS):
            group = range(g0, g0 + STACKED_HEADS)
            per_head = lambda ref: [ref[rows, head_cols[hh]] for hh in group]
            outs, st_new = _hgrn_chunk(per_head(q_ref), per_head(k_ref), per_head(v_ref), per_head(lf_ref),
                                       [st_ref[hh] for hh in group], levels)
            for idx, hh in enumerate(group):
                o_ref[rows, head_cols[hh]] = outs[idx]
                st_ref[hh] = st_new[idx]
        return carry

    lax.fori_loop(0, nchunks, body, 0)

    @pl.when(tb == nblocks - 1)
    def _():
        for hh in range(HEADS_PER_STEP):
            s_ref[hh] = st_ref[hh].T


def _hgrn_prompt(q, k, lf, v, *, batch, seq, tblock, chunk):
    nblocks = seq // tblock
    width = HEADS_PER_STEP * HG_DK
    tok = pl.BlockSpec((tblock, width), lambda b, h, t: (b * nblocks + t, h))
    return pl.pallas_call(
        functools.partial(_hgrn_prompt_kernel, chunk=chunk, nchunks=tblock // chunk, nblocks=nblocks),
        out_shape=(jax.ShapeDtypeStruct((batch * seq, D_MODEL), F32),
                   jax.ShapeDtypeStruct((batch, HG_HEADS, HG_DK, HG_DK), F32)),
        grid=(batch, HG_HEADS // HEADS_PER_STEP, nblocks),
        in_specs=[tok, tok, tok, tok],
        out_specs=(tok, pl.BlockSpec((None, HEADS_PER_STEP, HG_DK, HG_DK), lambda b, h, t: (b, h, 0, 0))),
        scratch_shapes=[pltpu.VMEM((HEADS_PER_STEP, HG_DK, HG_DK), F32)],
        compiler_params=pltpu.CompilerParams(
            dimension_semantics=("arbitrary", "arbitrary", "arbitrary"), vmem_limit_bytes=VMEM_LIMIT),
        name="hgrn_prompt_scan",
    )(q, k, lf, v)


def _hgrn_sample_kernel(q_ref, k_ref, lf_ref, v_ref, s0_ref, o_ref, s1_ref):
    levels = _pair_levels(SUBLANES, STACKED_HEADS)

    for g0 in range(0, HG_HEADS, STACKED_HEADS):
        heads = range(g0, g0 + STACKED_HEADS)
        per_head = lambda ref: [ref[0, h] for h in heads]
        outs, st_new = _hgrn_chunk(per_head(q_ref), per_head(k_ref), per_head(v_ref), per_head(lf_ref),
                                   [s0_ref[h].T for h in heads], levels)
        for idx, h in enumerate(heads):
            o_ref[0, h] = outs[idx]
            s1_ref[h] = st_new[idx].T


def _hgrn_sample(q, k, lf, v, state):
    dec_batch = q.shape[0]
    tok = pl.BlockSpec((1, HG_HEADS, SUBLANES, HG_DK), lambda b: (b, 0, 0, 0))
    st = pl.BlockSpec((None, HG_HEADS, HG_DK, HG_DK), lambda b: (b, 0, 0, 0))
    return pl.pallas_call(
        _hgrn_sample_kernel,
        out_shape=(jax.ShapeDtypeStruct(q.shape, F32), jax.ShapeDtypeStruct(state.shape, F32)),
        grid=(dec_batch,),
        in_specs=[tok, tok, tok, tok, st],
        out_specs=(tok, st),
        compiler_params=pltpu.CompilerParams(
            dimension_semantics=("arbitrary",), vmem_limit_bytes=VMEM_LIMIT),
        name="hgrn_sample_step",
    )(q, k, lf, v, state)


def _rope_tables(pos):
    inv = jnp.power(ROPE_THETA, -jnp.arange(0, ROPE_DIM, 2, dtype=F32) / ROPE_DIM)
    ang = pos.astype(F32)[:, None] * inv[None, :]
    cos, sin = jnp.cos(ang), jnp.sin(ang)
    n = pos.shape[0]
    pad1 = jnp.ones((n, LANES - ROPE_DIM), F32)
    pad0 = jnp.zeros((n, LANES - ROPE_DIM), F32)
    zero_h = jnp.zeros_like(sin)
    cos_t = jnp.concatenate([cos, cos, pad1], axis=1)
    sa_t = jnp.concatenate([zero_h, sin, pad0], axis=1)
    sb_t = jnp.concatenate([-sin, zero_h, pad0], axis=1)
    return cos_t, sa_t, sb_t


def kernel(x_prompt, x_sample, cache_k, cache_v, state_hgrn, page_table, attn_norm, attn_w_in, attn_q_norm,
           attn_k_norm, attn_lambda, attn_subln, attn_w_out, hgrn_norm, hgrn_w_in, hgrn_lower_bounds,
           hgrn_out_norm, hgrn_w_out):
    batch, seq, _ = x_prompt.shape
    dec_batch, dec_seq, _ = x_sample.shape
    mp, ms = batch * seq, dec_batch * dec_seq
    xp = x_prompt.reshape(mp, D_MODEL)
    xs = x_sample.reshape(ms, D_MODEL)

    lq1, lk1, lq2, lk2 = attn_lambda[0].astype(F32)
    lam = (jnp.exp(jnp.sum(lq1 * lk1)) - jnp.exp(jnp.sum(lq2 * lk2)) + LAMBDA_INIT_0).reshape(1)
    w_out = attn_w_out[0].astype(BF16)
    norm_g = attn_norm[0].reshape(1, D_MODEL)
    subln_g = attn_subln[0].reshape(1, DA_V_DIM)

    tabs_p = _rope_tables(jnp.arange(seq))
    tabs_s = _rope_tables(jnp.tile(PAST_LEN + jnp.arange(dec_seq), dec_batch))

    q_s, k_s, _, v_s, _, g_s, *w_in = _attn_proj(xs, norm_g, (attn_w_in[0],), attn_q_norm[0], attn_k_norm[0],
                                                 *tabs_s, tm=ms, tn=512)
    q_p, k_p, kb_p, v_p, vb_p, g_p = _attn_proj(xp, norm_g, w_in, attn_q_norm[0], attn_k_norm[0], *tabs_p,
                                                tm=1024, tn=512)
    og_p = _flash_attention(lam, q_p, kb_p, vb_p, g_p, subln_g, batch=batch, seq=seq, tq=1024)
    xp = _out_proj(og_p, w_out, xp, tm=512)

    q5 = q_s.reshape(dec_batch, dec_seq, DA_HEADS, 2, DA_HEAD_DIM).transpose(0, 3, 2, 1, 4)
    zeros = jnp.zeros_like(q5[:, 0])
    qm = jnp.stack([jnp.concatenate([q5[:, 0], zeros], axis=-1),
                    jnp.concatenate([zeros, q5[:, 1]], axis=-1)], axis=1)
    qm = qm.reshape(dec_batch, SAMPLE_ROWS, DA_V_DIM)
    rows_new = dec_seq * DA_HEADS
    gate_r = g_s.reshape(dec_batch, dec_seq, DA_HEADS, DA_V_DIM).transpose(0, 2, 1, 3)
    gate_r = gate_r.reshape(dec_batch, rows_new, DA_V_DIM)
    og_s = _sample_attention(page_table, lam, qm, cache_k, cache_v,
                             k_s.reshape(dec_batch, rows_new, DA_V_DIM),
                             v_s.reshape(dec_batch, rows_new, DA_V_DIM), gate_r, subln_g, dec_seq=dec_seq)
    og_s = og_s.reshape(dec_batch, DA_HEADS, dec_seq, DA_V_DIM).transpose(0, 2, 1, 3).reshape(ms, D_MODEL)
    xs = _out_proj(og_s, w_out, xs, tm=ms)

    k_prompt = k_p.reshape(1, batch, seq, DA_HEADS, 2 * DA_HEAD_DIM)
    v_prompt = v_p.reshape(1, batch, seq, DA_HEADS, DA_V_DIM)
    k_sample = k_s.reshape(1, dec_batch, dec_seq, DA_HEADS, 2 * DA_HEAD_DIM)
    v_sample = v_s.reshape(1, dec_batch, dec_seq, DA_HEADS, DA_V_DIM)

    pr = jax.nn.softmax(hgrn_lower_bounds.astype(F32), axis=0)
    lb = (jnp.cumsum(pr, axis=0)[1] - pr[0]).reshape(1, D_MODEL)
    hw_out = hgrn_w_out[0].astype(BF16)
    hnorm_g = hgrn_norm[0].reshape(1, D_MODEL)
    hout_g = hgrn_out_norm[0].reshape(1, D_MODEL)

    sq, sk, slf, sv, sg, *hw_in = _hgrn_proj(xs, hnorm_g, (hgrn_w_in[0],), lb, tm=ms, tn=512)
    hq, hk, hlf, hv, hg = _hgrn_proj(xp, hnorm_g, hw_in, lb, tm=1024, tn=512)
    ho, s_prompt = _hgrn_prompt(hq, hk, hlf, hv, batch=batch, seq=seq, tblock=512, chunk=64)
    xp = _hgrn_out(ho, hg, hout_g, hw_out, xp, tm=512)

    def to_heads(a):
        a = a.reshape(dec_batch, dec_seq, HG_HEADS, HG_DK).transpose(0, 2, 1, 3)
        return jnp.pad(a, ((0, 0), (0, 0), (0, SUBLANES - dec_seq), (0, 0)))

    so, s_sample = _hgrn_sample(to_heads(sq), to_heads(sk), to_heads(slf), to_heads(sv), state_hgrn[0])
    so = so[:, :, :dec_seq].transpose(0, 2, 1, 3).reshape(ms, D_MODEL)
    xs = _hgrn_out(so, sg, hout_g, hw_out, xs, tm=ms)

    return (xp.reshape(batch, seq, D_MODEL), xs.reshape(dec_batch, dec_seq, D_MODEL),
            k_prompt, v_prompt, k_sample, v_sample, s_prompt[None], s_sample[None])
```

```python
import functools
import math

import jax
import jax.numpy as jnp
from jax import lax
from jax.experimental import pallas as pl
from jax.experimental.pallas import tpu as pltpu

F32 = jnp.float32
BF16 = jnp.bfloat16

D_MODEL = 2048
PAST_LEN = 8192
PAGE_SIZE = 128
DA_HEADS = 8
DA_HEAD_DIM = 128
DA_V_DIM = 256
ROPE_DIM = 32
ROPE_THETA = 500000.0
HG_HEADS = 16
HG_DK = 128
EPS = 1e-6
LAMBDA_INIT_0 = 0.8 - 0.6 * math.exp(-0.3 * 0)

LANES = 128
SUBLANES = 8
VMEM_LIMIT = 56 * 1024 * 1024
NEG_BIG = -1e30
LOG2E = math.log2(math.e)
EXP2_SCALE = DA_HEAD_DIM ** -0.5 * LOG2E

NT_DIMS = (((1,), (1,)), ((), ()))
TN_DIMS = (((0,), (0,)), ((), ()))


def _silu(x):
    return x * (1.0 / (1.0 + jnp.exp(-x)))


def _sigmoid(x):
    return 1.0 / (1.0 + jnp.exp(-x))


def _group_map(group, nblk):
    return lambda i, j: (i, jnp.clip(j - group * nblk, 0, nblk - 1))


def _norm_rows(x_ref, g_ref, xn_ref):
    x = x_ref[...]
    ms = jnp.mean(x * x, axis=-1, keepdims=True)
    xn_ref[...] = (x * lax.rsqrt(ms + EPS) * g_ref[...]).astype(BF16)


ROW_SPLITS = 8
MIN_SLAB_ROWS = 128


def _paired_slabs(xn_ref, wa_ref, wb_ref, emit_a, emit_b):
    rows = min(xn_ref.shape[0], max(MIN_SLAB_ROWS, xn_ref.shape[0] // ROW_SPLITS))
    for r in range(xn_ref.shape[0] // rows):
        sl = slice(r * rows, (r + 1) * rows)
        emit_a(sl, jnp.dot(xn_ref[sl, :], wa_ref[...], preferred_element_type=F32))
        emit_b(sl, jnp.dot(xn_ref[sl, :], wb_ref[...], preferred_element_type=F32))


def _proj_weights(wa_ref, wb_ref, rest):
    if len(rest) == 1:
        return wa_ref, wb_ref, rest[0]
    wa_out, wb_out, xn_ref = rest
    wa_out[...] = wa_ref[...].astype(BF16)
    wb_out[...] = wb_ref[...].astype(BF16)
    return wa_out, wb_out, xn_ref


def _attn_proj_kernel(x_ref, g_ref, wa_ref, wb_ref, qg_ref, kg_ref, cos_ref, sa_ref, sb_ref,
                      q_ref, k_ref, kb_ref, v_ref, vb_ref, gate_ref, *rest, tn, nblk):
    wa_ref, wb_ref, xn_ref = _proj_weights(wa_ref, wb_ref, rest)
    j = pl.program_id(1)

    @pl.when(j == 0)
    def _():
        _norm_rows(x_ref, g_ref, xn_ref)

    def qk_post(acc, sl, gain_ref, emit):
        for idx in range(tn // LANES):
            c = idx % 2
            a = acc[:, idx * LANES:(idx + 1) * LANES]
            ms = jnp.mean(a * a, axis=-1, keepdims=True)
            y = a * lax.rsqrt(ms + EPS) * gain_ref[c:c + 1, :]
            y = (y * cos_ref[sl, :] + pltpu.roll(y, ROPE_DIM // 2, 1) * sa_ref[sl, :]
                 + pltpu.roll(y, LANES - ROPE_DIM // 2, 1) * sb_ref[sl, :])
            emit(slice(idx * LANES, (idx + 1) * LANES), y)

    @pl.when(j < nblk)
    def _():
        def emit_q(sl, acc):
            def put(cols, y):
                q_ref[sl, cols] = (y * EXP2_SCALE).astype(BF16)
            qk_post(acc, sl, qg_ref, put)

        def emit_v(sl, acc):
            v_ref[sl, :] = acc
            vb_ref[sl, :] = acc.astype(BF16)
        _paired_slabs(xn_ref, wa_ref, wb_ref, emit_q, emit_v)

    @pl.when(j >= nblk)
    def _():
        def emit_k(sl, acc):
            def put(cols, y):
                k_ref[sl, cols] = y
                kb_ref[sl, cols] = y.astype(BF16)
            qk_post(acc, sl, kg_ref, put)

        def emit_gate(sl, acc):
            gate_ref[sl, :] = acc
        _paired_slabs(xn_ref, wa_ref, wb_ref, emit_k, emit_gate)


def _proj_weight_specs(weights, tn, nblk):
    half = pl.BlockSpec((D_MODEL, tn), lambda i, j: (0, j))
    if len(weights) == 2:
        return list(weights), [half, half], (), ()
    (w,) = weights
    second_half = pl.BlockSpec((D_MODEL, tn), lambda i, j: (0, 2 * nblk + j))
    half_sd = jax.ShapeDtypeStruct((D_MODEL, w.shape[1] // 2), BF16)
    return [w, w], [half, second_half], (half_sd, half_sd), (half, half)


def _attn_proj(x, norm_g, weights, q_g, k_g, cos_t, sa_t, sb_t, *, tm, tn):
    m = x.shape[0]
    nblk = D_MODEL // tn
    w_args, w_specs, w_out_sd, w_out_specs = _proj_weight_specs(weights, tn, nblk)
    assert not w_out_sd or m == tm
    tab_blocks = cos_t.shape[0] // tm
    tab_spec = pl.BlockSpec((tm, LANES), lambda i, j: (i % tab_blocks, 0))
    small = lambda shape: pl.BlockSpec(shape, lambda i, j: (0, 0))
    out_sd = lambda dt: jax.ShapeDtypeStruct((m, D_MODEL), dt)
    out_spec = lambda g: pl.BlockSpec((tm, tn), _group_map(g, nblk))
    return pl.pallas_call(
        functools.partial(_attn_proj_kernel, tn=tn, nblk=nblk),
        out_shape=(out_sd(BF16), out_sd(F32), out_sd(BF16), out_sd(F32), out_sd(BF16), out_sd(F32)) + w_out_sd,
        grid=(m // tm, 2 * nblk),
        in_specs=[pl.BlockSpec((tm, D_MODEL), lambda i, j: (i, 0)), small((1, D_MODEL))] + w_specs
        + [small((2, LANES)), small((2, LANES)), tab_spec, tab_spec, tab_spec],
        out_specs=(out_spec(0), out_spec(1), out_spec(1), out_spec(0), out_spec(0), out_spec(1)) + w_out_specs,
        scratch_shapes=[pltpu.VMEM((tm, D_MODEL), BF16)],
        compiler_params=pltpu.CompilerParams(
            dimension_semantics=("arbitrary", "arbitrary"), vmem_limit_bytes=VMEM_LIMIT),
        name="attn_proj",
    )(x, norm_g, *w_args, q_g, k_g, cos_t, sa_t, sb_t)


def _hgrn_proj_kernel(x_ref, g_ref, wa_ref, wb_ref, lb_ref, q_ref, k_ref, lf_ref, i_ref, gate_ref, *rest, nblk):
    wa_ref, wb_ref, xn_ref = _proj_weights(wa_ref, wb_ref, rest)
    j = pl.program_id(1)

    @pl.when(j == 0)
    def _():
        _norm_rows(x_ref, g_ref, xn_ref)

    @pl.when(j < nblk)
    def _():
        def emit_q(sl, acc):
            q_ref[sl, :] = _silu(acc)

        def emit_i(sl, acc):
            i_ref[sl, :] = acc
        _paired_slabs(xn_ref, wa_ref, wb_ref, emit_q, emit_i)

    @pl.when(j >= nblk)
    def _():
        def emit_f(sl, acc):
            lb = lb_ref[...]
            fg = lb + (1.0 - lb) * _sigmoid(acc)
            k_ref[sl, :] = 1.0 - fg
            lf_ref[sl, :] = jnp.log(fg)

        def emit_gate(sl, acc):
            gate_ref[sl, :] = acc
        _paired_slabs(xn_ref, wa_ref, wb_ref, emit_f, emit_gate)


def _hgrn_proj(x, norm_g, weights, lb, *, tm, tn):
    m = x.shape[0]
    nblk = D_MODEL // tn
    w_args, w_specs, w_out_sd, w_out_specs = _proj_weight_specs(weights, tn, nblk)
    assert not w_out_sd or m == tm
    out_sd = jax.ShapeDtypeStruct((m, D_MODEL), F32)
    out_spec = lambda g: pl.BlockSpec((tm, tn), _group_map(g, nblk))
    return pl.pallas_call(
        functools.partial(_hgrn_proj_kernel, nblk=nblk),
        out_shape=(out_sd,) * 5 + w_out_sd,
        grid=(m // tm, 2 * nblk),
        in_specs=[pl.BlockSpec((tm, D_MODEL), lambda i, j: (i, 0)), pl.BlockSpec((1, D_MODEL), lambda i, j: (0, 0))]
        + w_specs + [pl.BlockSpec((1, tn), lambda i, j: (0, jnp.clip(j - nblk, 0, nblk - 1)))],
        out_specs=(out_spec(0), out_spec(1), out_spec(1), out_spec(0), out_spec(1)) + w_out_specs,
        scratch_shapes=[pltpu.VMEM((tm, D_MODEL), BF16)],
        compiler_params=pltpu.CompilerParams(
            dimension_semantics=("arbitrary", "arbitrary"), vmem_limit_bytes=VMEM_LIMIT),
        name="hgrn_proj",
    )(x, norm_g, *w_args, lb)


def _out_weights(w_ref, published):
    if not published:
        return w_ref
    published[0][...] = w_ref[...].astype(BF16)
    return published[0]


def _out_proj_kernel(a_ref, w_ref, x_ref, y_ref, *published):
    w_ref = _out_weights(w_ref, published)
    y_ref[...] = x_ref[...] + jnp.dot(a_ref[...], w_ref[...], preferred_element_type=F32)


def _out_call(kernel_fn, name, row_inputs_before, extra, w, x, tm):
    m = x.shape[0]
    publish = w.dtype != BF16
    assert not publish or m == tm
    row = pl.BlockSpec((tm, D_MODEL), lambda i: (i, 0))
    w_in_spec = pl.BlockSpec((D_MODEL, D_MODEL), lambda i: (0, 0), pipeline_mode=pl.Buffered(1))
    w_out_spec = pl.BlockSpec((D_MODEL, D_MODEL), lambda i: (0, 0))
    y_sd = jax.ShapeDtypeStruct((m, D_MODEL), F32)
    return pl.pallas_call(
        kernel_fn,
        out_shape=(y_sd, jax.ShapeDtypeStruct(w.shape, BF16)) if publish else y_sd,
        grid=(m // tm,),
        in_specs=[row] * len(row_inputs_before) + [pl.BlockSpec((1, D_MODEL), lambda i: (0, 0))] * len(extra)
        + [w_in_spec, row],
        out_specs=(row, w_out_spec) if publish else row,
        compiler_params=pltpu.CompilerParams(
            dimension_semantics=("arbitrary",), vmem_limit_bytes=VMEM_LIMIT),
        name=name,
    )(*row_inputs_before, *extra, w, x)


def _out_proj(a_bf, w, x, *, tm):
    return _out_call(_out_proj_kernel, "attn_out_proj", [a_bf], [], w, x, tm)


def _hgrn_out_kernel(o_ref, gate_ref, g_ref, w_ref, x_ref, y_ref, *published):
    w_ref = _out_weights(w_ref, published)
    o = o_ref[...]
    ms = jnp.mean(o * o, axis=-1, keepdims=True)
    a = (o * lax.rsqrt(ms + EPS) * g_ref[...]) * _silu(gate_ref[...])
    y_ref[...] = x_ref[...] + jnp.dot(a.astype(BF16), w_ref[...], preferred_element_type=F32)


def _hgrn_out(o, gate, out_g, w, x, *, tm):
    return _out_call(_hgrn_out_kernel, "hgrn_out_proj", [o, gate], [out_g], w, x, tm)


def _lane_tiles(x):
    return [x[:, j * LANES:(j + 1) * LANES] for j in range(x.shape[1] // LANES)]


def _subln_gate(o, gate, sg_ref):
    ms = jnp.mean(o * o, axis=-1, keepdims=True)
    o = (o * lax.rsqrt(ms + EPS) * sg_ref[...]) * (1.0 - LAMBDA_INIT_0)
    return o * _silu(gate)


def _flash_kernel(lam_ref, q_ref, k_ref, v_ref, gate_ref, sg_ref, o_ref, m0, m1, l0, l1, a0, a1, *, tq):
    m_refs, l_refs, acc_refs = (m0, m1), (l0, l1), (a0, a1)
    qi = pl.program_id(2)
    for c in range(2):
        m_refs[c][...] = jnp.full(m_refs[c].shape, NEG_BIG, F32)
        l_refs[c][...] = jnp.zeros(l_refs[c].shape, F32)
        acc_refs[c][...] = jnp.zeros(acc_refs[c].shape, F32)

    def step(ki, masked):
        kv_rows = pl.ds(pl.multiple_of(ki * tq, tq), tq)
        v_bf = v_ref[kv_rows, :]
        comps = [slice(c * LANES, (c + 1) * LANES) for c in range(2)]
        scores = [lax.dot_general(q_ref[:, comp], k_ref[kv_rows, comp], NT_DIMS, preferred_element_type=F32)
                  for comp in comps]
        probs, alphas = [], []
        for c in range(2):
            s = scores[c]
            if masked:
                r = lax.broadcasted_iota(jnp.int32, (tq, tq), 0)
                col = lax.broadcasted_iota(jnp.int32, (tq, tq), 1)
                s = jnp.where(col <= r, s, NEG_BIG)
            tiles = _lane_tiles(s)
            m_prev = m_refs[c][...]
            m_new = jnp.maximum(m_prev, jnp.max(functools.reduce(jnp.maximum, tiles), axis=-1, keepdims=True))
            alpha = jnp.exp2(m_prev - m_new)
            p_tiles = [jnp.exp2(t - m_new) for t in tiles]
            l_refs[c][...] = alpha * l_refs[c][...] + functools.reduce(jnp.add, p_tiles)
            m_refs[c][...] = m_new
            probs.append(jnp.concatenate(p_tiles, axis=-1).astype(BF16))
            alphas.append(alpha)
        pv = jnp.dot(jnp.concatenate(probs, axis=0), v_bf, preferred_element_type=F32)
        for c in range(2):
            acc_refs[c][...] = (jnp.concatenate([alphas[c], alphas[c]], axis=-1) * acc_refs[c][...]
                                + pv[c * tq:(c + 1) * tq, :])

    def body(ki, carry):
        step(ki, False)
        return carry

    lax.fori_loop(0, qi, body, 0)
    step(qi, True)

    norm = [acc_refs[c][...] / jnp.sum(l_refs[c][...], axis=-1, keepdims=True) for c in range(2)]
    o = norm[0] - lam_ref[0] * norm[1]
    o_ref[...] = _subln_gate(o, gate_ref[...], sg_ref).astype(BF16)


def _flash_attention(lam, q_bf, k_bf, v_bf, gate, subln_g, *, batch, seq, tq):
    nq = seq // tq
    qmap = lambda b, h, qi: (b * nq + qi, h)
    kvmap = lambda b, h, qi: (b, h)
    stat = pltpu.VMEM((tq, LANES), F32)
    acc = pltpu.VMEM((tq, DA_V_DIM), F32)
    return pl.pallas_call(
        functools.partial(_flash_kernel, tq=tq),
        out_shape=jax.ShapeDtypeStruct((batch * seq, DA_HEADS * DA_V_DIM), BF16),
        grid=(batch, DA_HEADS, nq),
        in_specs=[pl.BlockSpec(memory_space=pltpu.SMEM),
                  pl.BlockSpec((tq, DA_V_DIM), qmap),
                  pl.BlockSpec((seq, DA_V_DIM), kvmap),
                  pl.BlockSpec((seq, DA_V_DIM), kvmap),
                  pl.BlockSpec((tq, DA_V_DIM), qmap),
                  pl.BlockSpec((1, DA_V_DIM), lambda b, h, qi: (0, 0))],
        out_specs=pl.BlockSpec((tq, DA_V_DIM), qmap),
        scratch_shapes=[stat, stat, stat, stat, acc, acc],
        compiler_params=pltpu.CompilerParams(
            dimension_semantics=("arbitrary", "arbitrary", "arbitrary"), vmem_limit_bytes=VMEM_LIMIT),
        name="flash_diff_attn",
    )(lam, q_bf, k_bf, v_bf, gate, subln_g)


SAMPLE_ROWS = 64
PAGE_ROWS = PAGE_SIZE * DA_HEADS
PAGES_PER_STEP = 8


def _sample_attn_kernel(pt_ref, lam_ref, qm_ref, *rest, n_steps):
    g = PAGES_PER_STEP
    k_refs, v_refs = rest[:g], rest[g:2 * g]
    bias_ref, kn_ref, vn_ref, biasn_ref, gate_ref, sg_ref, o_ref, m_ref, l_ref, acc_ref = rest[2 * g:]
    step = pl.program_id(1)

    @pl.when(step == 0)
    def _():
        m_ref[...] = jnp.full(m_ref.shape, NEG_BIG, F32)
        l_ref[...] = jnp.zeros(l_ref.shape, F32)
        acc_ref[...] = jnp.zeros(acc_ref.shape, F32)

    def scores(kf, bias):
        return lax.dot_general(qm_ref[0], kf.astype(BF16), NT_DIMS, preferred_element_type=F32) + bias

    flat = lambda ref: ref[0, 0].reshape(PAGE_ROWS, DA_V_DIM)
    parts = []
    for i in range(g):
        tiles = _lane_tiles(scores(flat(k_refs[i]), bias_ref[...]))
        m_i = jnp.broadcast_to(jnp.max(functools.reduce(jnp.maximum, tiles), axis=-1, keepdims=True),
                               (SAMPLE_ROWS, LANES))
        p_tiles = [jnp.exp2(t - m_i) for t in tiles]
        pv_i = jnp.dot(jnp.concatenate(p_tiles, axis=-1).astype(BF16), flat(v_refs[i]).astype(BF16),
                       preferred_element_type=F32)
        parts.append((m_i, functools.reduce(jnp.add, p_tiles), pv_i))
    m_prev = m_ref[...]
    m_new = functools.reduce(jnp.maximum, [m_prev] + [part[0] for part in parts])
    alpha = jnp.exp2(m_prev - m_new)
    l_new = alpha * l_ref[...]
    acc_new = jnp.concatenate([alpha, alpha], axis=-1) * acc_ref[...]
    for m_i, l_i, pv_i in parts:
        w = jnp.exp2(m_i - m_new)
        l_new = l_new + w * l_i
        acc_new = acc_new + jnp.concatenate([w, w], axis=-1) * pv_i
    l_ref[...] = l_new
    acc_ref[...] = acc_new
    m_ref[...] = m_new

    @pl.when(step == n_steps - 1)
    def _():
        m_old = m_ref[:, 0:1]
        l_old = jnp.sum(l_ref[...], axis=-1, keepdims=True)
        s = scores(kn_ref[0], biasn_ref[...])
        m_fin = jnp.maximum(m_old, jnp.max(s, axis=-1, keepdims=True))
        a_fin = jnp.exp2(m_old - m_fin)
        p = jnp.exp2(s - m_fin)
        l_fin = a_fin * l_old + jnp.sum(p, axis=-1, keepdims=True)
        acc = a_fin * acc_ref[...] + jnp.dot(p.astype(BF16), vn_ref[0].astype(BF16), preferred_element_type=F32)
        norm = acc / l_fin
        half = SAMPLE_ROWS // 2
        o = norm[0:half, :] - lam_ref[0] * norm[half:, :]
        o_ref[0] = _subln_gate(o, gate_ref[0], sg_ref).astype(BF16)


def _sample_attention(page_table, lam, qm, cache_k, cache_v, k_new, v_new, gate_r, subln_g, *, dec_seq):
    dec_batch, n_pages = page_table.shape
    g = PAGES_PER_STEP
    n_steps = n_pages // g
    rows_new = dec_seq * DA_HEADS
    row_head = (jnp.arange(SAMPLE_ROWS) // dec_seq) % DA_HEADS
    col = jnp.arange(PAGE_ROWS)
    bias = jnp.where((col % DA_HEADS)[None, :] == row_head[:, None], 0.0, NEG_BIG).astype(F32)
    col_n = jnp.arange(rows_new)
    valid_n = ((col_n % DA_HEADS)[None, :] == row_head[:, None]) & (
        (col_n // DA_HEADS)[None, :] <= (jnp.arange(SAMPLE_ROWS) % dec_seq)[:, None])
    bias_n = jnp.where(valid_n, 0.0, NEG_BIG).astype(F32)

    def page_spec(i):
        return pl.BlockSpec((1, 1, PAGE_SIZE, DA_HEADS, DA_V_DIM),
                            lambda b, p, pt: (0, pt[b, p * g + i], 0, 0, 0))

    per_b = lambda rows, cols: pl.BlockSpec((1, rows, cols), lambda b, p, pt: (b, 0, 0))
    const = lambda shape: pl.BlockSpec(shape, lambda b, p, pt: (0, 0))
    grid_spec = pltpu.PrefetchScalarGridSpec(
        num_scalar_prefetch=1,
        grid=(dec_batch, n_steps),
        in_specs=[pl.BlockSpec(memory_space=pltpu.SMEM), per_b(SAMPLE_ROWS, DA_V_DIM)]
        + [page_spec(i) for i in range(g)] + [page_spec(i) for i in range(g)]
        + [const((SAMPLE_ROWS, PAGE_ROWS)), per_b(rows_new, DA_V_DIM), per_b(rows_new, DA_V_DIM),
           const((SAMPLE_ROWS, rows_new)), per_b(rows_new, DA_V_DIM), const((1, DA_V_DIM))],
        out_specs=per_b(rows_new, DA_V_DIM),
        scratch_shapes=[pltpu.VMEM((SAMPLE_ROWS, LANES), F32), pltpu.VMEM((SAMPLE_ROWS, LANES), F32),
                        pltpu.VMEM((SAMPLE_ROWS, DA_V_DIM), F32)],
    )
    return pl.pallas_call(
        functools.partial(_sample_attn_kernel, n_steps=n_steps),
        out_shape=jax.ShapeDtypeStruct((dec_batch, rows_new, DA_V_DIM), BF16),
        grid_spec=grid_spec,
        compiler_params=pltpu.CompilerParams(
            dimension_semantics=("arbitrary", "arbitrary"), vmem_limit_bytes=VMEM_LIMIT),
        name="paged_diff_attn",
    )(page_table, lam, qm, *([cache_k] * g), *([cache_v] * g), bias, k_new, v_new, bias_n, gate_r, subln_g)


def _tile_cumsum(x, rowi):
    for s in (1, 2, 4):
        x = x + jnp.where(rowi >= s, pltpu.roll(x, s, 0), 0.0)
    return x


def _pair_levels(length, nheads):
    n = length * nheads
    ri = lax.broadcasted_iota(jnp.int32, (n, n), 0)
    ci = lax.broadcasted_iota(jnp.int32, (n, n), 1)
    lvl = jnp.zeros((n, n), jnp.int32)
    shift = 0
    while (1 << shift) < length:
        lvl = lvl + jnp.where((ri >> shift) != (ci >> shift), 1, 0)
        shift += 1
    return jnp.where(((ri >> shift) != (ci >> shift)) | (ci > ri), -1, lvl)


def _hgrn_chunk(q, k, v, lf, states, levels):
    nheads = len(states)
    length = q[0].shape[0]
    nt = length // SUBLANES
    rowi = lax.broadcasted_iota(jnp.int32, (SUBLANES, LANES), 0)
    tile = lambda a, r: a[r * SUBLANES:(r + 1) * SUBLANES, :]
    stack = lambda tiles: jnp.concatenate(tiles, axis=0) if len(tiles) > 1 else tiles[0]
    bcast_row = lambda a, r, n: jnp.broadcast_to(a[r:r + 1, :], (n, LANES))

    b_tiles, b, o_inter, st_new = [], [], [], []
    for h in range(nheads):
        tiles, carry = [], None
        for r in range(nt):
            c = _tile_cumsum(tile(lf[h], r), rowi)
            if carry is not None:
                c = c + carry
            carry = c[SUBLANES - 1:SUBLANES, :]
            tiles.append(c)
        b_tiles.append(tiles)
        b.append(stack(tiles))
        b_last = carry
        qe = (q[h] * jnp.exp(b[h])).astype(BF16)
        o_inter.append(lax.dot_general(qe, states[h].astype(BF16), NT_DIMS, preferred_element_type=F32))
        kd = (k[h] * jnp.exp(b_last - b[h])).astype(BF16)
        st_new.append(states[h] * jnp.exp(b_last)
                      + lax.dot_general(v[h].astype(BF16), kd, TN_DIMS, preferred_element_type=F32))

    a_mat = jnp.where(levels == 0,
                      lax.dot_general(stack(q).astype(BF16), stack(k).astype(BF16), NT_DIMS,
                                      preferred_element_type=F32), 0.0)
    level, half = 1, 1
    while half < length:
        blk = 2 * half
        q_s, k_s = [], []
        for h in range(nheads):
            if half == 1:
                ref_b = stack([jnp.where(rowi % 2 == 1, pltpu.roll(bt, 1, 0), bt) for bt in b_tiles[h]])
            elif blk < SUBLANES:
                ref_b = stack([jnp.where(rowi < blk, bcast_row(bt, half - 1, SUBLANES),
                                         bcast_row(bt, blk + half - 1, SUBLANES)) for bt in b_tiles[h]])
            else:
                ref_b = stack([bcast_row(b[h], b0 + half - 1, blk) for b0 in range(0, length, blk)])
            e = jnp.exp2(jnp.abs(b[h] - ref_b) * -LOG2E)
            q_s.append(q[h] * e)
            k_s.append(k[h] * e)
        s = lax.dot_general(stack(q_s).astype(BF16), stack(k_s).astype(BF16), NT_DIMS,
                            preferred_element_type=F32)
        a_mat = jnp.where(levels == level, s, a_mat)
        level, half = level + 1, blk
    o_intra = jnp.dot(a_mat.astype(BF16), stack(v).astype(BF16), preferred_element_type=F32)
    outs = [o_inter[h] + o_intra[h * length:(h + 1) * length, :] for h in range(nheads)]
    return outs, st_new


HEADS_PER_STEP = 16
STACKED_HEADS = 4


def _hgrn_prompt_kernel(q_ref, k_ref, lf_ref, v_ref, o_ref, s_ref, st_ref, *, chunk, nchunks, nblocks):
    tb = pl.program_id(2)

    @pl.when(tb == 0)
    def _():
        st_ref[...] = jnp.zeros(st_ref.shape, F32)

    levels = _pair_levels(chunk, STACKED_HEADS)
    head_cols = [slice(hh * HG_DK, (hh + 1) * HG_DK) for hh in range(HEADS_PER_STEP)]

    def body(ci, carry):
        rows = pl.ds(pl.multiple_of(ci * chunk, chunk), chunk)
        for g0 in range(0, HEADS_PER_STEP, STACKED_HEADS):
            group = range(g0, g0 + STACKED_HEADS)
            per_head = lambda ref: [ref[rows, head_cols[hh]] for hh in group]
            outs, st_new = _hgrn_chunk(per_head(q_ref), per_head(k_ref), per_head(v_ref), per_head(lf_ref),
                                       [st_ref[hh] for hh in group], levels)
            for idx, hh in enumerate(group):
                o_ref[rows, head_cols[hh]] = outs[idx]
                st_ref[hh] = st_new[idx]
        return carry

    lax.fori_loop(0, nchunks, body, 0)

    @pl.when(tb == nblocks - 1)
    def _():
        for hh in range(HEADS_PER_STEP):
            s_ref[hh] = st_ref[hh].T


def _hgrn_prompt(q, k, lf, v, *, batch, seq, tblock, chunk):
    nblocks = seq // tblock
    width = HEADS_PER_STEP * HG_DK
    tok = pl.BlockSpec((tblock, width), lambda b, h, t: (b * nblocks + t, h))
    return pl.pallas_call(
        functools.partial(_hgrn_prompt_kernel, chunk=chunk, nchunks=tblock // chunk, nblocks=nblocks),
        out_shape=(jax.ShapeDtypeStruct((batch * seq, D_MODEL), F32),
                   jax.ShapeDtypeStruct((batch, HG_HEADS, HG_DK, HG_DK), F32)),
        grid=(batch, HG_HEADS // HEADS_PER_STEP, nblocks),
        in_specs=[tok, tok, tok, tok],
        out_specs=(tok, pl.BlockSpec((None, HEADS_PER_STEP, HG_DK, HG_DK), lambda b, h, t: (b, h, 0, 0))),
        scratch_shapes=[pltpu.VMEM((HEADS_PER_STEP, HG_DK, HG_DK), F32)],
        compiler_params=pltpu.CompilerParams(
            dimension_semantics=("arbitrary", "arbitrary", "arbitrary"), vmem_limit_bytes=VMEM_LIMIT),
        name="hgrn_prompt_scan",
    )(q, k, lf, v)


def _hgrn_sample_kernel(q_ref, k_ref, lf_ref, v_ref, s0_ref, o_ref, s1_ref):
    levels = _pair_levels(SUBLANES, STACKED_HEADS)

    for g0 in range(0, HG_HEADS, STACKED_HEADS):
        heads = range(g0, g0 + STACKED_HEADS)
        per_head = lambda ref: [ref[0, h] for h in heads]
        outs, st_new = _hgrn_chunk(per_head(q_ref), per_head(k_ref), per_head(v_ref), per_head(lf_ref),
                                   [s0_ref[h].T for h in heads], levels)
        for idx, h in enumerate(heads):
            o_ref[0, h] = outs[idx]
            s1_ref[h] = st_new[idx].T


def _hgrn_sample(q, k, lf, v, state):
    dec_batch = q.shape[0]
    tok = pl.BlockSpec((1, HG_HEADS, SUBLANES, HG_DK), lambda b: (b, 0, 0, 0))
    st = pl.BlockSpec((None, HG_HEADS, HG_DK, HG_DK), lambda b: (b, 0, 0, 0))
    return pl.pallas_call(
        _hgrn_sample_kernel,
        out_shape=(jax.ShapeDtypeStruct(q.shape, F32), jax.ShapeDtypeStruct(state.shape, F32)),
        grid=(dec_batch,),
        in_specs=[tok, tok, tok, tok, st],
        out_specs=(tok, st),
        compiler_params=pltpu.CompilerParams(
            dimension_semantics=("arbitrary",), vmem_limit_bytes=VMEM_LIMIT),
        name="hgrn_sample_step",
    )(q, k, lf, v, state)


def _rope_tables(pos):
    inv = jnp.power(ROPE_THETA, -jnp.arange(0, ROPE_DIM, 2, dtype=F32) / ROPE_DIM)
    ang = pos.astype(F32)[:, None] * inv[None, :]
    cos, sin = jnp.cos(ang), jnp.sin(ang)
    n = pos.shape[0]
    pad1 = jnp.ones((n, LANES - ROPE_DIM), F32)
    pad0 = jnp.zeros((n, LANES - ROPE_DIM), F32)
    zero_h = jnp.zeros_like(sin)
    cos_t = jnp.concatenate([cos, cos, pad1], axis=1)
    sa_t = jnp.concatenate([zero_h, sin, pad0], axis=1)
    sb_t = jnp.concatenate([-sin, zero_h, pad0], axis=1)
    return cos_t, sa_t, sb_t


def kernel(x_prompt, x_sample, cache_k, cache_v, state_hgrn, page_table, attn_norm, attn_w_in, attn_q_norm,
           attn_k_norm, attn_lambda, attn_subln, attn_w_out, hgrn_norm, hgrn_w_in, hgrn_lower_bounds,
           hgrn_out_norm, hgrn_w_out):
    batch, seq, _ = x_prompt.shape
    dec_batch, dec_seq, _ = x_sample.shape
    mp, ms = batch * seq, dec_batch * dec_seq
    xp = x_prompt.reshape(mp, D_MODEL)
    xs = x_sample.reshape(ms, D_MODEL)

    lq1, lk1, lq2, lk2 = attn_lambda[0].astype(F32)
    lam = (jnp.exp(jnp.sum(lq1 * lk1)) - jnp.exp(jnp.sum(lq2 * lk2)) + LAMBDA_INIT_0).reshape(1)
    norm_g = attn_norm[0].reshape(1, D_MODEL)
    subln_g = attn_subln[0].reshape(1, DA_V_DIM)

    tabs_p = _rope_tables(jnp.arange(seq))
    tabs_s = _rope_tables(jnp.tile(PAST_LEN + jnp.arange(dec_seq), dec_batch))

    q_s, k_s, _, v_s, _, g_s, *w_in = _attn_proj(xs, norm_g, (attn_w_in[0],), attn_q_norm[0], attn_k_norm[0],
                                                 *tabs_s, tm=ms, tn=512)
    q_p, k_p, kb_p, v_p, vb_p, g_p = _attn_proj(xp, norm_g, w_in, attn_q_norm[0], attn_k_norm[0], *tabs_p,
                                                tm=1024, tn=512)
    og_p = _flash_attention(lam, q_p, kb_p, vb_p, g_p, subln_g, batch=batch, seq=seq, tq=1024)

    q5 = q_s.reshape(dec_batch, dec_seq, DA_HEADS, 2, DA_HEAD_DIM).transpose(0, 3, 2, 1, 4)
    zeros = jnp.zeros_like(q5[:, 0])
    qm = jnp.stack([jnp.concatenate([q5[:, 0], zeros], axis=-1),
                    jnp.concatenate([zeros, q5[:, 1]], axis=-1)], axis=1)
    qm = qm.reshape(dec_batch, SAMPLE_ROWS, DA_V_DIM)
    rows_new = dec_seq * DA_HEADS
    gate_r = g_s.reshape(dec_batch, dec_seq, DA_HEADS, DA_V_DIM).transpose(0, 2, 1, 3)
    gate_r = gate_r.reshape(dec_batch, rows_new, DA_V_DIM)
    og_s = _sample_attention(page_table, lam, qm, cache_k, cache_v,
                             k_s.reshape(dec_batch, rows_new, DA_V_DIM),
                             v_s.reshape(dec_batch, rows_new, DA_V_DIM), gate_r, subln_g, dec_seq=dec_seq)
    og_s = og_s.reshape(dec_batch, DA_HEADS, dec_seq, DA_V_DIM).transpose(0, 2, 1, 3).reshape(ms, D_MODEL)
    xs, w_out = _out_proj(og_s, attn_w_out[0], xs, tm=ms)
    xp = _out_proj(og_p, w_out, xp, tm=512)

    k_prompt = k_p.reshape(1, batch, seq, DA_HEADS, 2 * DA_HEAD_DIM)
    v_prompt = v_p.reshape(1, batch, seq, DA_HEADS, DA_V_DIM)
    k_sample = k_s.reshape(1, dec_batch, dec_seq, DA_HEADS, 2 * DA_HEAD_DIM)
    v_sample = v_s.reshape(1, dec_batch, dec_seq, DA_HEADS, DA_V_DIM)

    pr = jax.nn.softmax(hgrn_lower_bounds.astype(F32), axis=0)
    lb = (jnp.cumsum(pr, axis=0)[1] - pr[0]).reshape(1, D_MODEL)
    hnorm_g = hgrn_norm[0].reshape(1, D_MODEL)
    hout_g = hgrn_out_norm[0].reshape(1, D_MODEL)

    sq, sk, slf, sv, sg, *hw_in = _hgrn_proj(xs, hnorm_g, (hgrn_w_in[0],), lb, tm=ms, tn=512)
    hq, hk, hlf, hv, hg = _hgrn_proj(xp, hnorm_g, hw_in, lb, tm=1024, tn=512)
    ho, s_prompt = _hgrn_prompt(hq, hk, hlf, hv, batch=batch, seq=seq, tblock=512, chunk=64)

    def to_heads(a):
        a = a.reshape(dec_batch, dec_seq, HG_HEADS, HG_DK).transpose(0, 2, 1, 3)
        return jnp.pad(a, ((0, 0), (0, 0), (0, SUBLANES - dec_seq), (0, 0)))

    so, s_sample = _hgrn_sample(to_heads(sq), to_heads(sk), to_heads(slf), to_heads(sv), state_hgrn[0])
    so = so[:, :, :dec_seq].transpose(0, 2, 1, 3).reshape(ms, D_MODEL)
    xs, hw_out = _hgrn_out(so, sg, hout_g, hgrn_w_out[0], xs, tm=ms)
    xp = _hgrn_out(ho, hg, hout_g, hw_out, xp, tm=512)

    return (xp.reshape(batch, seq, D_MODEL), xs.reshape(dec_batch, dec_seq, D_MODEL),
            k_prompt, v_prompt, k_sample, v_sample, s_prompt[None], s_sample[None])
```
